```python
import jax
import jax.numpy as jnp
from jax import lax
import numpy as np

D_MODEL = 1024
BATCH = 1
SEQ = 16384
DEPTH = 2

N_MIXERS = 2
N_LAYERS_A = (DEPTH + 1) // 2
N_LAYERS_B = DEPTH // 2

A_HEADS = 16
A_HEAD_DIM = D_MODEL // A_HEADS
A_GROUPS = ((128, 1), (512, 4), (2048, 16))
A_N_GROUPS = len(A_GROUPS)
A_PAD = A_GROUPS[-1][0]
Q_BLOCK = 128

B_INNER = 2 * D_MODEL
B_HEADS = 4
B_HEAD_DIM = B_INNER // B_HEADS
B_CONV = 4
B_QKV_BLOCK = 4
B_CHUNK = 128

D_FF = 4 * D_MODEL
PLE_DIM = 256

EPS = 1e-6
NEG_INF = -1e30

kernel_name = 'hybrid_dilated_attn_mlstm_trunk'


def rms_norm(x, g):
    xf = x.astype(jnp.float32)
    y = xf * lax.rsqrt(jnp.mean(xf * xf, axis=-1, keepdims=True) + EPS)
    return (y * g.astype(jnp.float32)).astype(x.dtype)


def alibi_slopes(n):
    return jnp.asarray([2.0 ** (-8.0 * (h + 1) / n) for h in range(n)], jnp.float32)


def dilated_attention(h, w_qkv, q_gain, k_gain, w_o):
    B, S, _ = h.shape
    qkv = (h @ w_qkv).reshape(B, S, A_N_GROUPS, 3, A_HEADS, A_HEAD_DIM)
    q = rms_norm(qkv[:, :, :, 0], q_gain[:, None, :])
    k = rms_norm(qkv[:, :, :, 1], k_gain[:, None, :])
    v = qkv[:, :, :, 2]
    pad = ((0, 0), (A_PAD, 0), (0, 0), (0, 0))
    k_pads = [jnp.pad(k[:, :, g], pad) for g in range(A_N_GROUPS)]
    v_pads = [jnp.pad(v[:, :, g], pad) for g in range(A_N_GROUPS)]
    slopes = alibi_slopes(A_HEADS)
    scale = A_HEAD_DIM ** -0.5
    n_blocks = S // Q_BLOCK
    q_blocks = jnp.moveaxis(q.reshape(B, n_blocks, Q_BLOCK, A_N_GROUPS, A_HEADS, A_HEAD_DIM), 1, 0)

    def block(args):
        blk, qb = args
        t = blk * Q_BLOCK + jnp.arange(Q_BLOCK)
        lses, outs = [], []
        for g, (window, dil) in enumerate(A_GROUPS):
            n_keys = window // dil + 1
            dist = jnp.arange(n_keys) * dil
            pos = t[:, None] - dist[None, :]
            kg = jnp.take(k_pads[g], pos + A_PAD, axis=1)
            vg = jnp.take(v_pads[g], pos + A_PAD, axis=1)
            s = jnp.einsum('bqhd,bqjhd->bhqj', qb[:, :, g], kg).astype(jnp.float32) * scale
            s = s - slopes[:, None, None] * dist.astype(jnp.float32)
            s = jnp.where((pos >= 0)[None, None], s, NEG_INF)
            lse = jax.nn.logsumexp(s, axis=-1)
            pr = jnp.exp(s - lse[..., None]).astype(vg.dtype)
            outs.append(jnp.einsum('bhqj,bqjhd->bqhd', pr, vg))
            lses.append(lse)
        wgt = jax.nn.softmax(jnp.stack(lses, axis=0), axis=0)
        return jnp.einsum('gbhq,gbqhd->bqhd', wgt.astype(outs[0].dtype), jnp.stack(outs, axis=0))

    o = lax.map(block, (jnp.arange(n_blocks), q_blocks))
    o = jnp.moveaxis(o, 0, 1).reshape(B, S, D_MODEL)
    return o @ w_o


def causal_depthwise_conv(x, w, b):
    y = lax.conv_general_dilated(
        x, w[:, None, :].astype(x.dtype), window_strides=(1,),
        padding=((B_CONV - 1, 0),), dimension_numbers=('NWC', 'WIO', 'NWC'),
        feature_group_count=x.shape[-1])
    return y + b


def block_diag_proj(x, w):
    B, S, C = x.shape
    nb, blk, _ = w.shape
    return jnp.einsum('bsnj,njk->bsnk', x.reshape(B, S, nb, blk), w).reshape(B, S, C)


def mlstm_chunkwise(q, k, v, i_pre, f_pre):
    B, NH, S, DH = q.shape
    L = B_CHUNK
    NC = S // L
    k = k * DH ** -0.5
    logf = jax.nn.log_sigmoid(f_pre)

    def to_chunks(a):
        return jnp.moveaxis(a.reshape(B, NH, NC, L, *a.shape[3:]), 2, 0)

    causal = jnp.tril(jnp.ones((L, L), dtype=bool))

    def step(carry, inp):
        C, n, m = carry
        qt, kt, vt, it, lf = inp
        b = jnp.cumsum(lf, axis=-1)
        d_mat = jnp.where(causal, b[..., :, None] - b[..., None, :] + it[..., None, :], NEG_INF)
        inter = b + m[..., None]
        m_t = jnp.maximum(inter, jnp.max(d_mat, axis=-1))
        s = jnp.einsum('bhtd,bhsd->bhts', qt, kt) * jnp.exp(d_mat - m_t[..., None])
        sc = jnp.exp(inter - m_t)
        num = sc[..., None] * jnp.einsum('bhtd,bhde->bhte', qt, C) + jnp.einsum('bhts,bhse->bhte', s, vt)
        den = sc * jnp.einsum('bhtd,bhd->bht', qt, n) + jnp.sum(s, axis=-1)
        h = num / jnp.maximum(jnp.abs(den), jnp.exp(-m_t))[..., None]
        b_last = b[..., -1]
        g = b_last[..., None] - b + it
        m_new = jnp.maximum(b_last + m, jnp.max(g, axis=-1))
        decay = jnp.exp(b_last + m - m_new)
        wk = kt * jnp.exp(g - m_new[..., None])[..., None]
        C_new = decay[..., None, None] * C + jnp.einsum('bhsd,bhse->bhde', wk, vt)
        n_new = decay[..., None] * n + jnp.sum(wk, axis=2)
        return (C_new, n_new, m_new), h

    init = (jnp.zeros((B, NH, DH, DH), jnp.float32),
            jnp.zeros((B, NH, DH), jnp.float32),
            jnp.full((B, NH), NEG_INF, jnp.float32))
    _, hs = lax.scan(step, init, (to_chunks(q), to_chunks(k), to_chunks(v), to_chunks(i_pre), to_chunks(logf)))
    return jnp.moveaxis(hs, 0, 2).reshape(B, NH, S, DH)


def mlstm_mixer(h, w_up, conv_w, conv_b, w_q, w_k, w_v, w_gate, b_gate, h_gain, skip, w_down):
    B, S, _ = h.shape
    xz = h @ w_up
    xm, z = xz[..., :B_INNER], xz[..., B_INNER:]
    xc = jax.nn.silu(causal_depthwise_conv(xm, conv_w, conv_b))
    q = block_diag_proj(xc, w_q)
    k = block_diag_proj(xc, w_k)
    v = block_diag_proj(xm, w_v)
    gates = (jnp.concatenate([q, k, v], axis=-1) @ w_gate).astype(jnp.float32) + b_gate.astype(jnp.float32)
    i_pre = jnp.transpose(gates[..., :B_HEADS], (0, 2, 1))
    f_pre = jnp.transpose(gates[..., B_HEADS:], (0, 2, 1))

    def heads(a):
        return jnp.transpose(a.reshape(B, S, B_HEADS, B_HEAD_DIM), (0, 2, 1, 3)).astype(jnp.float32)

    hc = mlstm_chunkwise(heads(q), heads(k), heads(v), i_pre, f_pre)
    hc = rms_norm(jnp.transpose(hc, (0, 2, 1, 3)), h_gain.reshape(B_HEADS, B_HEAD_DIM))
    hc = hc.reshape(B, S, B_INNER).astype(h.dtype)
    out = (hc + skip * xc) * jax.nn.silu(z)
    return out @ w_down


def squared_relu_mlp(h, w1, w2):
    return jnp.square(jax.nn.relu(h @ w1)) @ w2


def setup_inputs(seed: int = 0) -> dict:
    key = jax.random.key(seed)
    keys = jax.random.split(key, 32)
    counter = [0]

    def nrm(shape, scale):
        kk = keys[counter[0]]
        counter[0] += 1
        return scale * jax.random.normal(kk, shape, jnp.float32)

    def gain(shape):
        return 1.0 + nrm(shape, 0.01)

    NA, NB = N_LAYERS_A, N_LAYERS_B
    n_blk = B_INNER // B_QKV_BLOCK
    f_bias = jnp.linspace(3.0, 6.0, B_HEADS, dtype=jnp.float32)
    return {
        'x': nrm((BATCH, SEQ, D_MODEL), 1.0),
        'p': nrm((DEPTH, BATCH, SEQ, PLE_DIM), 1.0),
        'a_norm': gain((NA, D_MODEL)),
        'a_w_qkv': nrm((NA, D_MODEL, A_N_GROUPS * 3 * D_MODEL), D_MODEL ** -0.5),
        'a_q_gain': gain((NA, A_N_GROUPS, A_HEAD_DIM)),
        'a_k_gain': gain((NA, A_N_GROUPS, A_HEAD_DIM)),
        'a_w_o': nrm((NA, D_MODEL, D_MODEL), D_MODEL ** -0.5),
        'b_norm': gain((NB, D_MODEL)),
        'b_w_up': nrm((NB, D_MODEL, 2 * B_INNER), D_MODEL ** -0.5),
        'b_conv_w': nrm((NB, B_CONV, B_INNER), B_CONV ** -0.5),
        'b_conv_b': nrm((NB, B_INNER), 0.01),
        'b_w_q': nrm((NB, n_blk, B_QKV_BLOCK, B_QKV_BLOCK), B_QKV_BLOCK ** -0.5),
        'b_w_k': nrm((NB, n_blk, B_QKV_BLOCK, B_QKV_BLOCK), B_QKV_BLOCK ** -0.5),
        'b_w_v': nrm((NB, n_blk, B_QKV_BLOCK, B_QKV_BLOCK), B_QKV_BLOCK ** -0.5),
        'b_w_gate': nrm((NB, 3 * B_INNER, 2 * B_HEADS), (3 * B_INNER) ** -0.5),
        'b_b_gate': jnp.concatenate([nrm((NB, B_HEADS), 0.1), f_bias[None, :] + nrm((NB, B_HEADS), 0.01)], axis=-1),
        'b_h_gain': gain((NB, B_INNER)),
        'b_skip': gain((NB, B_INNER)),
        'b_w_down': nrm((NB, B_INNER, D_MODEL), B_INNER ** -0.5),
        'mlp_norm': gain((DEPTH, D_MODEL)),
        'mlp_w1': nrm((DEPTH, D_MODEL, D_FF), D_MODEL ** -0.5),
        'mlp_w2': nrm((DEPTH, D_FF, D_MODEL), D_FF ** -0.5),
        'ple_norm': gain((DEPTH, D_MODEL)),
        'ple_w_gate': nrm((DEPTH, D_MODEL, D_MODEL), D_MODEL ** -0.5),
        'ple_w_proj': nrm((DEPTH, PLE_DIM, D_MODEL), PLE_DIM ** -0.5),
    }


def reference(x, p, a_norm, a_w_qkv, a_q_gain, a_k_gain, a_w_o,
              b_norm, b_w_up, b_conv_w, b_conv_b, b_w_q, b_w_k, b_w_v,
              b_w_gate, b_b_gate, b_h_gain, b_skip, b_w_down,
              mlp_norm, mlp_w1, mlp_w2, ple_norm, ple_w_gate, ple_w_proj):
    for i in range(DEPTH):
        j = i // N_MIXERS
        if i % N_MIXERS == 0:
            x = x + dilated_attention(rms_norm(x, a_norm[j]), a_w_qkv[j], a_q_gain[j], a_k_gain[j], a_w_o[j])
        else:
            x = x + mlstm_mixer(rms_norm(x, b_norm[j]), b_w_up[j], b_conv_w[j], b_conv_b[j],
                                b_w_q[j], b_w_k[j], b_w_v[j], b_w_gate[j], b_b_gate[j],
                                b_h_gain[j], b_skip[j], b_w_down[j])
        x = x + squared_relu_mlp(rms_norm(x, mlp_norm[i]), mlp_w1[i], mlp_w2[i])
        gate = jax.nn.sigmoid(rms_norm(x, ple_norm[i]) @ ple_w_gate[i])
        x = x + gate * (p[i] @ ple_w_proj[i])
    return x
```

```python
import functools

import jax
import jax.numpy as jnp
from jax import lax
from jax.experimental import pallas as pl
from jax.experimental.pallas import tpu as pltpu

F32 = jnp.float32
BF16 = jnp.bfloat16

D_MODEL = 1024
A_HEADS = 16
A_HEAD_DIM = D_MODEL // A_HEADS
A_GROUPS = ((128, 1), (512, 4), (2048, 16))
A_N_GROUPS = len(A_GROUPS)
Q_BLOCK = 128
B_INNER = 2 * D_MODEL
B_HEADS = 4
B_HEAD_DIM = B_INNER // B_HEADS
B_CONV = 4
B_QKV_BLOCK = 4
B_CHUNK = 128
D_FF = 4 * D_MODEL
PLE_DIM = 256
EPS = 1e-6
NEG_INF = -1e30

MXU_TILE = 256
LANES = 128
VMEM_LIMIT_BYTES = 60000 * 1024
ROW_TILE = 512

NT_DIMS = (((1,), (1,)), ((), ()))
TN_DIMS = (((0,), (0,)), ((), ()))


def _params(n_axes):
    return pltpu.CompilerParams(dimension_semantics=("arbitrary",) * n_axes,
                                vmem_limit_bytes=VMEM_LIMIT_BYTES)


def _resident(shape):
    return pl.BlockSpec(shape, lambda *_: (0,) * len(shape), pipeline_mode=pl.Buffered(1))


def _rms(x, g):
    ms = jnp.mean(x * x, axis=-1, keepdims=True)
    return x * lax.rsqrt(ms + EPS) * g


def _qkv_kernel(x_ref, g_ref, w_ref, qg_ref, kg_ref, ones_ref, o_ref):
    hn = _rms(x_ref[...], g_ref[...]).astype(BF16)
    for c in range(3 * A_N_GROUPS):
        grp, kind = divmod(c, 3)
        cs = slice(c * D_MODEL, (c + 1) * D_MODEL)
        y = jnp.dot(hn, w_ref[:, cs], preferred_element_type=F32)
        if kind < 2:
            y2 = y * y
            hi = y2.astype(BF16)
            lo = (y2 - hi.astype(F32)).astype(BF16)
            parts = []
            for t in range(D_MODEL // MXU_TILE):
                ts = slice(t * MXU_TILE, (t + 1) * MXU_TILE)
                parts.append(jnp.dot(hi[:, ts], ones_ref[...], preferred_element_type=F32)
                             + jnp.dot(lo[:, ts], ones_ref[...], preferred_element_type=F32))
            ss = jnp.concatenate(parts, axis=1)
            gain = qg_ref[grp] if kind == 0 else kg_ref[grp]
            y = y * lax.rsqrt(ss * (1.0 / A_HEAD_DIM) + EPS) * gain
        o_ref[:, cs] = y.astype(BF16)


def _qkv_proj(x, g, w, qg, kg, ones):
    s = x.shape[0]
    n = w.shape[1]
    return pl.pallas_call(
        _qkv_kernel,
        grid=(s // ROW_TILE,),
        in_specs=[pl.BlockSpec((ROW_TILE, D_MODEL), lambda i: (i, 0)),
                  _resident(g.shape), _resident(w.shape), _resident(qg.shape),
                  _resident(kg.shape), _resident(ones.shape)],
        out_specs=pl.BlockSpec((ROW_TILE, n), lambda i: (i, 0)),
        out_shape=jax.ShapeDtypeStruct((s, n), BF16),
        compiler_params=_params(1),
        name="qkv_proj",
    )(x, g, w, qg, kg, ones)


def _attn_kernel(*refs, first, last):
    q_ref, kp_ref, kc_ref, vp_ref, vc_ref, bias_ref, hmask_ref = refs[:7]
    refs = refs[7:]
    if not first:
        m_in, l_in, acc_in = refs[:3]
        refs = refs[3:]
        m_prev_all = m_in[...]
        l_prev_all = l_in[...]
    if last:
        (o_ref,) = refs
    else:
        m_out, l_out, acc_out = refs
    sel = (pl.program_id(1) == 0).astype(jnp.int32)
    lane = lax.broadcasted_iota(jnp.int32, (Q_BLOCK, LANES), 1)
    lo = lane < A_HEAD_DIM
    m_tile = jnp.zeros((Q_BLOCK, LANES), F32)
    l_tile = jnp.zeros((Q_BLOCK, LANES), F32)
    for pair in range(A_HEADS // 2):
        cs = slice(pair * LANES, (pair + 1) * LANES)
        q2 = q_ref[:, cs]
        kcat = jnp.concatenate([kp_ref[:, cs], kc_ref[:, cs]], axis=0)
        vcat = jnp.concatenate([vp_ref[:, cs], vc_ref[:, cs]], axis=0)
        pv, alphas, ls = [], [], []
        for hh in range(2):
            h = 2 * pair + hh
            qh = q2 * hmask_ref[hh]
            s = lax.dot_general(qh, kcat, NT_DIMS, preferred_element_type=F32)
            s = s + bias_ref[sel, h]
            m_new = jnp.max(s, axis=-1, keepdims=True)
            if not first:
                m_prev = m_prev_all[:, h:h + 1]
                m_new = jnp.maximum(m_prev, m_new)
                alpha = jnp.exp(m_prev - m_new)
                alphas.append(alpha)
            p = jnp.exp(s - m_new)
            l_new = jnp.sum(p, axis=-1, keepdims=True)
            if not first:
                l_new = alpha * l_prev_all[:, h:h + 1] + l_new
            ls.append(l_new)
            pv.append(jnp.dot(p.astype(BF16), vcat, preferred_element_type=F32))
            if not last:
                m_tile = jnp.where(lane == h, m_new, m_tile)
                l_tile = jnp.where(lane == h, l_new, l_tile)
        acc = jnp.where(lo, pv[0], pv[1])
        if not first:
            acc = jnp.where(lo, alphas[0], alphas[1]) * acc_in[:, cs] + acc
        if last:
            o_ref[:, cs] = (acc / jnp.where(lo, ls[0], ls[1])).astype(BF16)
        else:
            acc_out[:, cs] = acc
    if not last:
        m_out[...] = m_tile
        l_out[...] = l_tile


def _attn_bias(dil):
    slopes = jnp.asarray([2.0 ** (-8.0 * (h + 1) / A_HEADS) for h in range(A_HEADS)], F32)
    row = jnp.arange(Q_BLOCK)[:, None]
    col = jnp.arange(Q_BLOCK)[None, :]

    def table(steps, valid):
        dist = (steps * dil).astype(F32)
        return jnp.where(valid[None], -(slopes[:, None, None] * dist[None]), NEG_INF)

    prev = table(Q_BLOCK + row - col, col >= row)
    cur = table(row - col, col <= row)
    normal = jnp.concatenate([prev, cur], axis=-1)
    firstb = jnp.concatenate([jnp.full_like(prev, NEG_INF), cur], axis=-1)
    return jnp.stack([normal, firstb])


def _attention_group(qkv, grp, dil, bias, hmask, state):
    s, n = qkv.shape
    sd = s // dil
    nb = sd // Q_BLOCK
    first = state is None
    last = grp == A_N_GROUPS - 1
    qkv_v = qkv.reshape(sd, dil * n)
    ncol = n // D_MODEL
    base = 3 * grp

    def col_spec(off, prev):
        if prev:
            return pl.BlockSpec((Q_BLOCK, D_MODEL),
                                lambda r, b: (jnp.maximum(b - 1, 0), r * ncol + base + off))
        return pl.BlockSpec((Q_BLOCK, D_MODEL), lambda r, b: (b, r * ncol + base + off))

    stat_spec = pl.BlockSpec((Q_BLOCK, LANES), lambda r, b: (b, r))
    acc_spec = pl.BlockSpec((Q_BLOCK, D_MODEL), lambda r, b: (b, r))
    in_specs = [col_spec(0, False), col_spec(1, True), col_spec(1, False),
                col_spec(2, True), col_spec(2, False),
                _resident(bias.shape), _resident(hmask.shape)]
    args = [qkv_v, qkv_v, qkv_v, qkv_v, qkv_v, bias, hmask]
    if not first:
        m, l, acc = state
        in_specs += [stat_spec, stat_spec, acc_spec]
        args += [m.reshape(sd, dil * LANES), l.reshape(sd, dil * LANES),
                 acc.reshape(sd, dil * D_MODEL)]
    if last:
        out_specs = acc_spec
        out_shape = jax.ShapeDtypeStruct((sd, dil * D_MODEL), BF16)
    else:
        out_specs = [stat_spec, stat_spec, acc_spec]
        out_shape = [jax.ShapeDtypeStruct((sd, dil * LANES), F32),
                     jax.ShapeDtypeStruct((sd, dil * LANES), F32),
                     jax.ShapeDtypeStruct((sd, dil * D_MODEL), F32)]
    out = pl.pallas_call(
        functools.partial(_attn_kernel, first=first, last=last),
        grid=(dil, nb),
        in_specs=in_specs,
        out_specs=out_specs,
        out_shape=out_shape,
        compiler_params=_params(2),
        name=f"dilated_attn_g{grp}",
    )(*args)
    if last:
        return out.reshape(s, D_MODEL)
    m, l, acc = out
    return m.reshape(s, LANES), l.reshape(s, LANES), acc.reshape(s, D_MODEL)


def _norm_matmul_kernel(x_ref, g_ref, w_ref, o_ref):
    hn = _rms(x_ref[...], g_ref[...]).astype(BF16)
    n = w_ref.shape[1]
    for c in range(n // D_MODEL):
        cs = slice(c * D_MODEL, (c + 1) * D_MODEL)
        o_ref[:, cs] = jnp.dot(hn, w_ref[:, cs], preferred_element_type=F32)


def _norm_matmul(x, g, w):
    s = x.shape[0]
    n = w.shape[1]
    return pl.pallas_call(
        _norm_matmul_kernel,
        grid=(s // ROW_TILE,),
        in_specs=[pl.BlockSpec((ROW_TILE, D_MODEL), lambda i: (i, 0)),
                  _resident(g.shape), _resident(w.shape)],
        out_specs=pl.BlockSpec((ROW_TILE, n), lambda i: (i, 0)),
        out_shape=jax.ShapeDtypeStruct((s, n), F32),
        compiler_params=_params(1),
        name="norm_matmul",
    )(x, g, w)


def _post_kernel(x_ref, mix_ref, wo_ref, gm_ref, w1_ref, w2_ref, gp_ref, wg_ref, p_ref, wp_ref,
                 o_ref):
    x = x_ref[...] + jnp.dot(mix_ref[...], wo_ref[...], preferred_element_type=F32)
    hn = _rms(x, gm_ref[...]).astype(BF16)
    acc = jnp.zeros_like(x)
    for c in range(D_FF // D_MODEL):
        cs = slice(c * D_MODEL, (c + 1) * D_MODEL)
        a = jnp.dot(hn, w1_ref[:, cs], preferred_element_type=F32)
        a = jnp.square(jnp.maximum(a, 0.0)).astype(BF16)
        acc = acc + jnp.dot(a, w2_ref[cs, :], preferred_element_type=F32)
    x = x + acc
    gate = jax.nn.sigmoid(jnp.dot(_rms(x, gp_ref[...]).astype(BF16), wg_ref[...],
                                  preferred_element_type=F32))
    emb = jnp.dot(p_ref[...].astype(BF16), wp_ref[...], preferred_element_type=F32)
    o_ref[...] = x + gate * emb


def _post_mixer(x, mix, wo, gm, w1, w2, gp, wg, p, wp):
    s = x.shape[0]
    kin = mix.shape[1]
    row = lambda width: pl.BlockSpec((ROW_TILE, width), lambda i: (i, 0))
    return pl.pallas_call(
        _post_kernel,
        grid=(s // ROW_TILE,),
        in_specs=[row(D_MODEL), row(kin), _resident(wo.shape), _resident(gm.shape),
                  _resident(w1.shape), _resident(w2.shape), _resident(gp.shape),
                  _resident(wg.shape), row(PLE_DIM), _resident(wp.shape)],
        out_specs=row(D_MODEL),
        out_shape=jax.ShapeDtypeStruct((s, D_MODEL), F32),
        compiler_params=_params(1),
        name="post_mixer",
    )(x, mix, wo, gm, w1, w2, gp, wg, p, wp)


HALO = 8


def _conv_qkv_kernel(xm_ref, halo_ref, cw_ref, cb_ref, wq_ref, wk_ref, wv_ref, wg_ref, bg_ref,
                     q_ref, k_ref, v_ref, xc_ref, g_ref):
    xm = xm_ref[...]
    halo = jnp.where(pl.program_id(0) == 0, 0.0, halo_ref[...])
    hrow = lax.broadcasted_iota(jnp.int32, halo.shape, 0)
    y = cb_ref[...] + cw_ref[B_CONV - 1:B_CONV, :] * xm
    for j in range(1, B_CONV):
        r = pltpu.roll(xm, j, 0)
        hr = pltpu.roll(halo, j, 0)
        top = jnp.where(hrow < j, hr, r[:HALO])
        shifted = jnp.concatenate([top, r[HALO:]], axis=0)
        y = y + cw_ref[B_CONV - 1 - j:B_CONV - j, :] * shifted
    xc = jax.nn.silu(y)
    xc_ref[...] = xc
    xcb = xc.astype(BF16)
    xmb = xm.astype(BF16)
    gates = jnp.broadcast_to(bg_ref[...], g_ref.shape)
    for t in range(B_INNER // MXU_TILE):
        ts = slice(t * MXU_TILE, (t + 1) * MXU_TILE)
        qf = jnp.dot(xcb[:, ts], wq_ref[t], preferred_element_type=F32)
        kf = jnp.dot(xcb[:, ts], wk_ref[t], preferred_element_type=F32)
        vf = jnp.dot(xmb[:, ts], wv_ref[t], preferred_element_type=F32)
        qb, kb, vb = qf.astype(BF16), kf.astype(BF16), vf.astype(BF16)
        q_ref[:, ts] = qb
        k_ref[:, ts] = (kf * B_HEAD_DIM ** -0.5).astype(BF16)
        v_ref[:, ts] = vb
        for part, val in enumerate((qb, kb, vb)):
            ws = slice(part * B_INNER + t * MXU_TILE, part * B_INNER + (t + 1) * MXU_TILE)
            gates = gates + jnp.dot(val, wg_ref[ws, :], preferred_element_type=F32)
    g_ref[...] = gates


def _conv_qkv(xz, cw, cb, wq, wk, wv, wg, bg):
    s = xz.shape[0]
    tiles_per_halo = ROW_TILE // HALO
    row = lambda: pl.BlockSpec((ROW_TILE, B_INNER), lambda i: (i, 0))
    return pl.pallas_call(
        _conv_qkv_kernel,
        grid=(s // ROW_TILE,),
        in_specs=[row(),
                  pl.BlockSpec((HALO, B_INNER),
                               lambda i: (jnp.maximum(i * tiles_per_halo - 1, 0), 0)),
                  _resident(cw.shape), _resident(cb.shape), _resident(wq.shape),
                  _resident(wk.shape), _resident(wv.shape), _resident(wg.shape),
                  _resident(bg.shape)],
        out_specs=[row(), row(), row(), row(),
                   pl.BlockSpec((ROW_TILE, LANES), lambda i: (i, 0))],
        out_shape=[jax.ShapeDtypeStruct((s, B_INNER), BF16)] * 3
        + [jax.ShapeDtypeStruct((s, B_INNER), F32), jax.ShapeDtypeStruct((s, LANES), F32)],
        compiler_params=_params(1),
        name="conv_qkv_gates",
    )(xz, xz, cw, cb, wq, wk, wv, wg, bg)


def _mlstm_kernel(q_ref, k_ref, v_ref, g_ref, xc_ref, z_ref, tri_ref, hg_ref, sk_ref, o_ref,
                  c_sc, n_sc, m_sc):
    @pl.when(pl.program_id(0) == 0)
    def _():
        c_sc[...] = jnp.zeros_like(c_sc)
        n_sc[...] = jnp.zeros_like(n_sc)
        m_sc[...] = jnp.full_like(m_sc, NEG_INF)

    g_all = g_ref[...]
    lf_all = jax.nn.log_sigmoid(g_all)
    b_all = jnp.dot(tri_ref[...], lf_all, preferred_element_type=F32,
                    precision=lax.Precision.HIGHEST)
    b_t = b_all.T
    g_t = g_all.T
    row = lax.broadcasted_iota(jnp.int32, (B_CHUNK, B_CHUNK), 0)
    col = lax.broadcasted_iota(jnp.int32, (B_CHUNK, B_CHUNK), 1)
    causal = col <= row
    for h in range(B_HEADS):
        hs = slice(h * B_HEAD_DIM, (h + 1) * B_HEAD_DIM)
        i_col = g_all[:, h:h + 1]
        b_col = b_all[:, B_HEADS + h:B_HEADS + h + 1]
        i_row = g_t[h:h + 1, :]
        b_row = b_t[B_HEADS + h:B_HEADS + h + 1, :]
        m_prev = m_sc[h, 0:1, 0:1]
        dmat = jnp.where(causal, b_col - b_row + i_row, NEG_INF)
        inter = b_col + m_prev
        m_t = jnp.maximum(inter, jnp.max(dmat, axis=-1, keepdims=True))
        qh = q_ref[:, hs]
        kh = k_ref[:, hs]
        vh = v_ref[:, hs]
        s = lax.dot_general(qh, kh, NT_DIMS, preferred_element_type=F32) * jnp.exp(dmat - m_t)
        sc = jnp.exp(inter - m_t)
        c_old = c_sc[h]
        num = (sc * jnp.dot(qh, c_old.astype(BF16), preferred_element_type=F32)
               + jnp.dot(s.astype(BF16), vh, preferred_element_type=F32))
        n_old = n_sc[h]
        den = (sc * jnp.sum(qh.astype(F32) * n_old, axis=-1, keepdims=True)
               + jnp.sum(s, axis=-1, keepdims=True))
        hv = num / jnp.maximum(jnp.abs(den), jnp.exp(-m_t))
        b_last = b_col[B_CHUNK - 1:B_CHUNK, :]
        g_col = b_last - b_col + i_col
        m_new = jnp.maximum(b_last + m_prev, jnp.max(g_col, axis=0, keepdims=True))
        decay = jnp.exp(b_last + m_prev - m_new)
        wk = kh.astype(F32) * jnp.exp(g_col - m_new)
        c_sc[h] = decay * c_old + lax.dot_general(wk.astype(BF16), vh, TN_DIMS,
                                                  preferred_element_type=F32)
        n_sc[h] = decay * n_old + jnp.sum(wk, axis=0, keepdims=True)
        m_sc[h] = jnp.broadcast_to(m_new, m_sc.shape[1:])
        hn = _rms(hv, hg_ref[:, hs])
        o_ref[:, hs] = ((hn + sk_ref[:, hs] * xc_ref[:, hs]) * jax.nn.silu(z_ref[:, hs])).astype(BF16)


def _mlstm_scan(q, k, v, gates, xc, xz, tri, hg, sk):
    s = q.shape[0]
    blk = lambda j=0: pl.BlockSpec((B_CHUNK, B_INNER), lambda c: (c, j))
    return pl.pallas_call(
        _mlstm_kernel,
        grid=(s // B_CHUNK,),
        in_specs=[blk(), blk(), blk(), pl.BlockSpec((B_CHUNK, LANES), lambda c: (c, 0)),
                  blk(), blk(1), _resident(tri.shape), _resident(hg.shape), _resident(sk.shape)],
        out_specs=blk(),
        out_shape=jax.ShapeDtypeStruct((s, B_INNER), BF16),
        scratch_shapes=[pltpu.VMEM((B_HEADS, B_HEAD_DIM, B_HEAD_DIM), F32),
                        pltpu.VMEM((B_HEADS, 1, B_HEAD_DIM), F32),
                        pltpu.VMEM((B_HEADS, 8, LANES), F32)],
        compiler_params=_params(1),
        name="mlstm_scan",
    )(q, k, v, gates, xc, xz, tri, hg, sk)


def _block_diag_tiles(w):
    nblk, blk, _ = w.shape
    per_tile = MXU_TILE // blk
    w = w.reshape(nblk // per_tile, per_tile, blk, blk)
    eye = jnp.eye(per_tile, dtype=w.dtype)
    dense = w[:, :, :, None, :] * eye[None, :, None, :, None]
    return dense.reshape(nblk // per_tile, MXU_TILE, MXU_TILE).astype(BF16)


def _row(v):
    return v.reshape(1, -1).astype(F32)


def kernel(x, p, a_norm, a_w_qkv, a_q_gain, a_k_gain, a_w_o, b_norm, b_w_up, b_conv_w, b_conv_b,
           b_w_q, b_w_k, b_w_v, b_w_gate, b_b_gate, b_h_gain, b_skip, b_w_down, mlp_norm, mlp_w1,
           mlp_w2, ple_norm, ple_w_gate, ple_w_proj):
    bsz, seq, _ = x.shape
    assert bsz == 1
    xs = x.reshape(seq, D_MODEL)

    qg = jnp.tile(a_q_gain[0], (1, A_HEADS)).reshape(A_N_GROUPS, 1, D_MODEL) * A_HEAD_DIM ** -0.5
    kg = jnp.tile(a_k_gain[0], (1, A_HEADS)).reshape(A_N_GROUPS, 1, D_MODEL)
    seg = jnp.arange(MXU_TILE) // A_HEAD_DIM
    ones = (seg[:, None] == seg[None, :]).astype(BF16)
    hmask = jnp.stack([jnp.broadcast_to(jnp.arange(LANES) < A_HEAD_DIM, (Q_BLOCK, LANES)),
                       jnp.broadcast_to(jnp.arange(LANES) >= A_HEAD_DIM, (Q_BLOCK, LANES))]
                      ).astype(BF16)
    qkv = _qkv_proj(xs, _row(a_norm[0]), a_w_qkv[0].astype(BF16), qg, kg, ones)
    state = None
    for grp, (_, dil) in enumerate(A_GROUPS):
        state = _attention_group(qkv, grp, dil, _attn_bias(dil), hmask, state)
    xs = _post_mixer(xs, state, a_w_o[0].astype(BF16), _row(mlp_norm[0]),
                     mlp_w1[0].astype(BF16), mlp_w2[0].astype(BF16), _row(ple_norm[0]),
                     ple_w_gate[0].astype(BF16), p[0, 0], ple_w_proj[0].astype(BF16))

    xz = _norm_matmul(xs, _row(b_norm[0]), b_w_up[0].astype(BF16))
    wg = jnp.zeros((3 * B_INNER, LANES), F32).at[:, :2 * B_HEADS].set(b_w_gate[0]).astype(BF16)
    bg = jnp.zeros((1, LANES), F32).at[0, :2 * B_HEADS].set(b_b_gate[0])
    q, k, v, xc, gates = _conv_qkv(xz, b_conv_w[0], _row(b_conv_b[0]), _block_diag_tiles(b_w_q[0]),
                                   _block_diag_tiles(b_w_k[0]), _block_diag_tiles(b_w_v[0]), wg, bg)
    tri = (jnp.arange(B_CHUNK)[:, None] >= jnp.arange(B_CHUNK)[None, :]).astype(F32)
    mix = _mlstm_scan(q, k, v, gates, xc, xz, tri, _row(b_h_gain[0]), _row(b_skip[0]))
    xs = _post_mixer(xs, mix, b_w_down[0].astype(BF16), _row(mlp_norm[1]),
                     mlp_w1[1].astype(BF16), mlp_w2[1].astype(BF16), _row(ple_norm[1]),
                     ple_w_gate[1].astype(BF16), p[1, 0], ple_w_proj[1].astype(BF16))
    return xs.reshape(bsz, seq, D_MODEL)
```

```python
import functools

import jax
import jax.numpy as jnp
from jax import lax
from jax.experimental import pallas as pl
from jax.experimental.pallas import tpu as pltpu

F32 = jnp.float32
BF16 = jnp.bfloat16

D_MODEL = 1024
A_HEADS = 16
A_HEAD_DIM = D_MODEL // A_HEADS
A_GROUPS = ((128, 1), (512, 4), (2048, 16))
A_N_GROUPS = len(A_GROUPS)
Q_BLOCK = 128
B_INNER = 2 * D_MODEL
B_HEADS = 4
B_HEAD_DIM = B_INNER // B_HEADS
B_CONV = 4
B_QKV_BLOCK = 4
B_CHUNK = 128
D_FF = 4 * D_MODEL
PLE_DIM = 256
EPS = 1e-6
NEG_INF = -1e30

MXU_TILE = 256
LANES = 128
N_SLABS = D_MODEL // LANES
VMEM_LIMIT_BYTES = 60000 * 1024
ROW_TILE = 512

NT_DIMS = (((1,), (1,)), ((), ()))
TN_DIMS = (((0,), (0,)), ((), ()))


def _params(n_axes):
    return pltpu.CompilerParams(dimension_semantics=("arbitrary",) * n_axes,
                                vmem_limit_bytes=VMEM_LIMIT_BYTES)


def _resident(shape):
    return pl.BlockSpec(shape, lambda *_: (0,) * len(shape), pipeline_mode=pl.Buffered(1))


def _rms(x, g):
    ms = jnp.mean(x * x, axis=-1, keepdims=True)
    return x * lax.rsqrt(ms + EPS) * g


def _split_bf16(v):
    hi = v.astype(BF16)
    return hi, (v - hi.astype(F32)).astype(BF16)


def _qkv_kernel(x_ref, g_ref, w_ref, qg_ref, kg_ref, ones_ref, *rest):
    out_refs, hn_sc = rest[:A_N_GROUPS], rest[A_N_GROUPS]
    hn = _rms(x_ref[...], g_ref[...])
    tm = hn.shape[0]
    for j in range(N_SLABS):
        hn_sc[j] = hn[:, j * LANES:(j + 1) * LANES]
    for grp, (_, dil) in enumerate(A_GROUPS):
        o_ref = out_refs[grp]
        n = tm // dil
        if dil == 1:
            hg = hn.astype(BF16)
        else:
            hg = jnp.concatenate(
                [jnp.concatenate([hn_sc[j, pl.ds(r, n, stride=dil), :] for j in range(N_SLABS)],
                                 axis=1) for r in range(dil)], axis=0).astype(BF16)
        for kind in range(3):
            c = 3 * grp + kind
            y = jnp.dot(hg, w_ref[:, c * D_MODEL:(c + 1) * D_MODEL], preferred_element_type=F32)
            if kind < 2:
                hi, lo = _split_bf16(y * y)
                parts = []
                for t in range(D_MODEL // MXU_TILE):
                    ts = slice(t * MXU_TILE, (t + 1) * MXU_TILE)
                    parts.append(jnp.dot(hi[:, ts], ones_ref[...], preferred_element_type=F32)
                                 + jnp.dot(lo[:, ts], ones_ref[...], preferred_element_type=F32))
                ss = jnp.concatenate(parts, axis=1)
                gain = qg_ref[grp] if kind == 0 else kg_ref[grp]
                y = y * lax.rsqrt(ss * (1.0 / A_HEAD_DIM) + EPS) * gain
            yb = y.astype(BF16)
            ks = slice(kind * D_MODEL, (kind + 1) * D_MODEL)
            for r in range(dil):
                o_ref[r, :, ks] = yb[r * n:(r + 1) * n]


def _qkv_proj(x, g, w, qg, kg, ones):
    s = x.shape[0]
    out_specs, out_shape = [], []
    for _, dil in A_GROUPS:
        out_specs.append(pl.BlockSpec((dil, ROW_TILE // dil, 3 * D_MODEL), lambda i: (0, i, 0)))
        out_shape.append(jax.ShapeDtypeStruct((dil, s // dil, 3 * D_MODEL), BF16))
    return pl.pallas_call(
        _qkv_kernel,
        grid=(s // ROW_TILE,),
        in_specs=[pl.BlockSpec((ROW_TILE, D_MODEL), lambda i: (i, 0)),
                  _resident(g.shape), _resident(w.shape), _resident(qg.shape),
                  _resident(kg.shape), _resident(ones.shape)],
        out_specs=out_specs,
        out_shape=out_shape,
        scratch_shapes=[pltpu.VMEM((N_SLABS, ROW_TILE, LANES), F32)],
        compiler_params=_params(1),
        name="qkv_proj",
    )(x, g, w, qg, kg, ones)


def _attn_kernel(q_ref, kp_ref, kc_ref, vp_ref, vc_ref, bias_ref, hmask_ref, o_ref, lse_ref, *,
                 dil):
    sel = (pl.program_id(0) == 0).astype(jnp.int32)
    res = pl.program_id(1)
    rows = pl.ds(res, Q_BLOCK, stride=dil) if dil > 1 else slice(None)
    lane = lax.broadcasted_iota(jnp.int32, (Q_BLOCK, LANES), 1)
    lo = lane < A_HEAD_DIM
    lse_tile = jnp.zeros((Q_BLOCK, LANES), F32)
    for pair in range(A_HEADS // 2):
        cs = slice(pair * LANES, (pair + 1) * LANES)
        q2 = q_ref[:, cs]
        kcat = jnp.concatenate([kp_ref[:, cs], kc_ref[:, cs]], axis=0)
        vcat = jnp.concatenate([vp_ref[:, cs], vc_ref[:, cs]], axis=0)
        outs = []
        for hh in range(2):
            h = 2 * pair + hh
            qh = q2 * hmask_ref[hh]
            s = lax.dot_general(qh, kcat, NT_DIMS, preferred_element_type=F32)
            s = s + bias_ref[sel, h]
            m = jnp.max(s, axis=-1, keepdims=True)
            p = jnp.exp(s - m)
            l = jnp.sum(p, axis=-1, keepdims=True)
            outs.append(jnp.dot(p.astype(BF16), vcat, preferred_element_type=F32) / l)
            lse_tile = jnp.where(lane == h, m + jnp.log(l), lse_tile)
        o_ref[pair, rows, :] = jnp.where(lo, outs[0], outs[1])
    lse_ref[rows, :] = lse_tile


def _attn_bias(dil):
    slopes = jnp.asarray([2.0 ** (-8.0 * (h + 1) / A_HEADS) for h in range(A_HEADS)], F32)
    row = jnp.arange(Q_BLOCK)[:, None]
    col = jnp.arange(Q_BLOCK)[None, :]

    def table(steps, valid):
        dist = (steps * dil).astype(F32)
        return jnp.where(valid[None], -(slopes[:, None, None] * dist[None]), NEG_INF)

    prev = table(Q_BLOCK + row - col, col >= row)
    cur = table(row - col, col <= row)
    normal = jnp.concatenate([prev, cur], axis=-1)
    firstb = jnp.concatenate([jnp.full_like(prev, NEG_INF), cur], axis=-1)
    return jnp.stack([normal, firstb])


def _attention_group(qkv_g, grp, dil, bias, hmask):
    _, sd, _ = qkv_g.shape
    s = sd * dil
    nb = sd // Q_BLOCK

    def col_spec(off, prev):
        if prev:
            return pl.BlockSpec((None, Q_BLOCK, D_MODEL),
                                lambda b, r: (r, jnp.maximum(b - 1, 0), off))
        return pl.BlockSpec((None, Q_BLOCK, D_MODEL), lambda b, r: (r, b, off))

    return pl.pallas_call(
        functools.partial(_attn_kernel, dil=dil),
        grid=(nb, dil),
        in_specs=[col_spec(0, False), col_spec(1, True), col_spec(1, False),
                  col_spec(2, True), col_spec(2, False),
                  _resident(bias.shape), _resident(hmask.shape)],
        out_specs=[pl.BlockSpec((N_SLABS, Q_BLOCK * dil, LANES), lambda b, r: (0, b, 0)),
                   pl.BlockSpec((Q_BLOCK * dil, LANES), lambda b, r: (b, 0))],
        out_shape=[jax.ShapeDtypeStruct((N_SLABS, s, LANES), F32),
                   jax.ShapeDtypeStruct((s, LANES), F32)],
        compiler_params=_params(2),
        name=f"dilated_attn_g{grp}",
    )(qkv_g, qkv_g, qkv_g, qkv_g, qkv_g, bias, hmask)


def _norm_matmul_kernel(x_ref, g_ref, w_ref, o_ref):
    hn = _rms(x_ref[...], g_ref[...]).astype(BF16)
    n = w_ref.shape[1]
    for c in range(n // D_MODEL):
        cs = slice(c * D_MODEL, (c + 1) * D_MODEL)
        o_ref[:, cs] = jnp.dot(hn, w_ref[:, cs], preferred_element_type=F32)


def _norm_matmul(x, g, w):
    s = x.shape[0]
    n = w.shape[1]
    return pl.pallas_call(
        _norm_matmul_kernel,
        grid=(s // ROW_TILE,),
        in_specs=[pl.BlockSpec((ROW_TILE, D_MODEL), lambda i: (i, 0)),
                  _resident(g.shape), _resident(w.shape)],
        out_specs=pl.BlockSpec((ROW_TILE, n), lambda i: (i, 0)),
        out_shape=jax.ShapeDtypeStruct((s, n), F32),
        compiler_params=_params(1),
        name="norm_matmul",
    )(x, g, w)


def _merge_groups(o_refs, lse_refs, expand_ref):
    lses = [r[...] for r in lse_refs]
    top = functools.reduce(jnp.maximum, lses)
    es = [jnp.exp(v - top) for v in lses]
    tot = functools.reduce(jnp.add, es)
    merged = None
    for o_ref, e in zip(o_refs, es):
        hi, lo = _split_bf16(e / tot)
        wide = (jnp.dot(hi, expand_ref[...], preferred_element_type=F32)
                + jnp.dot(lo, expand_ref[...], preferred_element_type=F32))
        o = jnp.concatenate([o_ref[j] for j in range(N_SLABS)], axis=1)
        merged = wide * o if merged is None else merged + wide * o
    return merged.astype(BF16)


def _post_kernel(*refs, merge):
    x_ref = refs[0]
    if merge:
        mix = _merge_groups(refs[1:1 + A_N_GROUPS], refs[1 + A_N_GROUPS:1 + 2 * A_N_GROUPS],
                            refs[1 + 2 * A_N_GROUPS])
        refs = refs[2 + 2 * A_N_GROUPS:]
    else:
        mix = refs[1][...]
        refs = refs[2:]
    wo_ref, gm_ref, w1_ref, w2_ref, gp_ref, wg_ref, p_ref, wp_ref, o_ref = refs
    x = x_ref[...] + jnp.dot(mix, wo_ref[...], preferred_element_type=F32)
    hn = _rms(x, gm_ref[...]).astype(BF16)
    acc = jnp.zeros_like(x)
    for c in range(D_FF // D_MODEL):
        cs = slice(c * D_MODEL, (c + 1) * D_MODEL)
        a = jnp.dot(hn, w1_ref[:, cs], preferred_element_type=F32)
        a = jnp.square(jnp.maximum(a, 0.0)).astype(BF16)
        acc = acc + jnp.dot(a, w2_ref[cs, :], preferred_element_type=F32)
    x = x + acc
    gate = jax.nn.sigmoid(jnp.dot(_rms(x, gp_ref[...]).astype(BF16), wg_ref[...],
                                  preferred_element_type=F32))
    emb = jnp.dot(p_ref[...].astype(BF16), wp_ref[...], preferred_element_type=F32)
    o_ref[...] = x + gate * emb


def _post_mixer(x, mix, wo, gm, w1, w2, gp, wg, p, wp):
    s = x.shape[0]
    row = lambda width: pl.BlockSpec((ROW_TILE, width), lambda i: (i, 0))
    merge = isinstance(mix, tuple)
    if merge:
        outs, lses, expand = mix
        mix_args = [*outs, *lses, expand]
        mix_specs = ([pl.BlockSpec((N_SLABS, ROW_TILE, LANES), lambda i: (0, i, 0))] * len(outs)
                     + [row(LANES)] * len(lses) + [_resident(expand.shape)])
    else:
        mix_args = [mix]
        mix_specs = [row(mix.shape[1])]
    return pl.pallas_call(
        functools.partial(_post_kernel, merge=merge),
        grid=(s // ROW_TILE,),
        in_specs=[row(D_MODEL), *mix_specs, _resident(wo.shape), _resident(gm.shape),
                  _resident(w1.shape), _resident(w2.shape), _resident(gp.shape),
                  _resident(wg.shape), row(PLE_DIM), _resident(wp.shape)],
        out_specs=row(D_MODEL),
        out_shape=jax.ShapeDtypeStruct((s, D_MODEL), F32),
        compiler_params=_params(1),
        name="post_mixer",
    )(x, *mix_args, wo, gm, w1, w2, gp, wg, p, wp)


HALO = 8


def _conv_qkv_kernel(xm_ref, halo_ref, cw_ref, cb_ref, wq_ref, wk_ref, wv_ref, wg_ref, bg_ref,
                     q_ref, k_ref, v_ref, xc_ref, g_ref):
    xm = xm_ref[...]
    halo = jnp.where(pl.program_id(0) == 0, 0.0, halo_ref[...])
    hrow = lax.broadcasted_iota(jnp.int32, halo.shape, 0)
    y = cb_ref[...] + cw_ref[B_CONV - 1:B_CONV, :] * xm
    for j in range(1, B_CONV):
        r = pltpu.roll(xm, j, 0)
        hr = pltpu.roll(halo, j, 0)
        top = jnp.where(hrow < j, hr, r[:HALO])
        shifted = jnp.concatenate([top, r[HALO:]], axis=0)
        y = y + cw_ref[B_CONV - 1 - j:B_CONV - j, :] * shifted
    xc = jax.nn.silu(y)
    xc_ref[...] = xc
    xcb = xc.astype(BF16)
    xmb = xm.astype(BF16)
    gates = jnp.broadcast_to(bg_ref[...], g_ref.shape)
    for t in range(B_INNER // MXU_TILE):
        ts = slice(t * MXU_TILE, (t + 1) * MXU_TILE)
        qf = jnp.dot(xcb[:, ts], wq_ref[t], preferred_element_type=F32)
        kf = jnp.dot(xcb[:, ts], wk_ref[t], preferred_element_type=F32)
        vf = jnp.dot(xmb[:, ts], wv_ref[t], preferred_element_type=F32)
        qb, kb, vb = qf.astype(BF16), kf.astype(BF16), vf.astype(BF16)
        q_ref[:, ts] = qb
        k_ref[:, ts] = (kf * B_HEAD_DIM ** -0.5).astype(BF16)
        v_ref[:, ts] = vb
        for part, val in enumerate((qb, kb, vb)):
            ws = slice(part * B_INNER + t * MXU_TILE, part * B_INNER + (t + 1) * MXU_TILE)
            gates = gates + jnp.dot(val, wg_ref[ws, :], preferred_element_type=F32)
    g_ref[...] = gates


def _conv_qkv(xz, cw, cb, wq, wk, wv, wg, bg):
    s = xz.shape[0]
    tiles_per_halo = ROW_TILE // HALO
    row = lambda: pl.BlockSpec((ROW_TILE, B_INNER), lambda i: (i, 0))
    return pl.pallas_call(
        _conv_qkv_kernel,
        grid=(s // ROW_TILE,),
        in_specs=[row(),
                  pl.BlockSpec((HALO, B_INNER),
                               lambda i: (jnp.maximum(i * tiles_per_halo - 1, 0), 0)),
                  _resident(cw.shape), _resident(cb.shape), _resident(wq.shape),
                  _resident(wk.shape), _resident(wv.shape), _resident(wg.shape),
                  _resident(bg.shape)],
        out_specs=[row(), row(), row(), row(),
                   pl.BlockSpec((ROW_TILE, LANES), lambda i: (i, 0))],
        out_shape=[jax.ShapeDtypeStruct((s, B_INNER), BF16)] * 3
        + [jax.ShapeDtypeStruct((s, B_INNER), F32), jax.ShapeDtypeStruct((s, LANES), F32)],
        compiler_params=_params(1),
        name="conv_qkv_gates",
    )(xz, xz, cw, cb, wq, wk, wv, wg, bg)


def _mlstm_kernel(q_ref, k_ref, v_ref, g_ref, xc_ref, z_ref, tri_ref, hg_ref, sk_ref, o_ref,
                  c_sc, n_sc, m_sc):
    @pl.when(pl.program_id(0) == 0)
    def _():
        c_sc[...] = jnp.zeros_like(c_sc)
        n_sc[...] = jnp.zeros_like(n_sc)
        m_sc[...] = jnp.full_like(m_sc, NEG_INF)

    g_all = g_ref[...]
    lf_all = jax.nn.log_sigmoid(g_all)
    b_all = jnp.dot(tri_ref[...], lf_all, preferred_element_type=F32,
                    precision=lax.Precision.HIGHEST)
    b_t = b_all.T
    g_t = g_all.T
    row = lax.broadcasted_iota(jnp.int32, (B_CHUNK, B_CHUNK), 0)
    col = lax.broadcasted_iota(jnp.int32, (B_CHUNK, B_CHUNK), 1)
    causal = col <= row
    for h in range(B_HEADS):
        hs = slice(h * B_HEAD_DIM, (h + 1) * B_HEAD_DIM)
        i_col = g_all[:, h:h + 1]
        b_col = b_all[:, B_HEADS + h:B_HEADS + h + 1]
        i_row = g_t[h:h + 1, :]
        b_row = b_t[B_HEADS + h:B_HEADS + h + 1, :]
        m_prev = m_sc[h, 0:1, 0:1]
        dmat = jnp.where(causal, b_col - b_row + i_row, NEG_INF)
        inter = b_col + m_prev
        m_t = jnp.maximum(inter, jnp.max(dmat, axis=-1, keepdims=True))
        qh = q_ref[:, hs]
        kh = k_ref[:, hs]
        vh = v_ref[:, hs]
        s = lax.dot_general(qh, kh, NT_DIMS, preferred_element_type=F32) * jnp.exp(dmat - m_t)
        sc = jnp.exp(inter - m_t)
        c_old = c_sc[h]
        num = (sc * jnp.dot(qh, c_old.astype(BF16), preferred_element_type=F32)
               + jnp.dot(s.astype(BF16), vh, preferred_element_type=F32))
        n_old = n_sc[h]
        den = (sc * jnp.sum(qh.astype(F32) * n_old, axis=-1, keepdims=True)
               + jnp.sum(s, axis=-1, keepdims=True))
        hv = num / jnp.maximum(jnp.abs(den), jnp.exp(-m_t))
        b_last = b_col[B_CHUNK - 1:B_CHUNK, :]
        g_col = b_last - b_col + i_col
        m_new = jnp.maximum(b_last + m_prev, jnp.max(g_col, axis=0, keepdims=True))
        decay = jnp.exp(b_last + m_prev - m_new)
        wk = kh.astype(F32) * jnp.exp(g_col - m_new)
        c_sc[h] = decay * c_old + lax.dot_general(wk.astype(BF16), vh, TN_DIMS,
                                                  preferred_element_type=F32)
        n_sc[h] = decay * n_old + jnp.sum(wk, axis=0, keepdims=True)
        m_sc[h] = jnp.broadcast_to(m_new, m_sc.shape[1:])
        hn = _rms(hv, hg_ref[:, hs])
        o_ref[:, hs] = ((hn + sk_ref[:, hs] * xc_ref[:, hs]) * jax.nn.silu(z_ref[:, hs])).astype(BF16)


def _mlstm_scan(q, k, v, gates, xc, xz, tri, hg, sk):
    s = q.shape[0]
    blk = lambda j=0: pl.BlockSpec((B_CHUNK, B_INNER), lambda c: (c, j))
    return pl.pallas_call(
        _mlstm_kernel,
        grid=(s // B_CHUNK,),
        in_specs=[blk(), blk(), blk(), pl.BlockSpec((B_CHUNK, LANES), lambda c: (c, 0)),
                  blk(), blk(1), _resident(tri.shape), _resident(hg.shape), _resident(sk.shape)],
        out_specs=blk(),
        out_shape=jax.ShapeDtypeStruct((s, B_INNER), BF16),
        scratch_shapes=[pltpu.VMEM((B_HEADS, B_HEAD_DIM, B_HEAD_DIM), F32),
                        pltpu.VMEM((B_HEADS, 1, B_HEAD_DIM), F32),
                        pltpu.VMEM((B_HEADS, 8, LANES), F32)],
        compiler_params=_params(1),
        name="mlstm_scan",
    )(q, k, v, gates, xc, xz, tri, hg, sk)


def _block_diag_tiles(w):
    nblk, blk, _ = w.shape
    per_tile = MXU_TILE // blk
    w = w.reshape(nblk // per_tile, per_tile, blk, blk)
    eye = jnp.eye(per_tile, dtype=w.dtype)
    dense = w[:, :, :, None, :] * eye[None, :, None, :, None]
    return dense.reshape(nblk // per_tile, MXU_TILE, MXU_TILE).astype(BF16)


def _row(v):
    return v.reshape(1, -1).astype(F32)


def kernel(x, p, a_norm, a_w_qkv, a_q_gain, a_k_gain, a_w_o, b_norm, b_w_up, b_conv_w, b_conv_b,
           b_w_q, b_w_k, b_w_v, b_w_gate, b_b_gate, b_h_gain, b_skip, b_w_down, mlp_norm, mlp_w1,
           mlp_w2, ple_norm, ple_w_gate, ple_w_proj):
    bsz, seq, _ = x.shape
    assert bsz == 1
    xs = x.reshape(seq, D_MODEL)

    qg = jnp.tile(a_q_gain[0], (1, A_HEADS)).reshape(A_N_GROUPS, 1, D_MODEL) * A_HEAD_DIM ** -0.5
    kg = jnp.tile(a_k_gain[0], (1, A_HEADS)).reshape(A_N_GROUPS, 1, D_MODEL)
    seg = jnp.arange(MXU_TILE) // A_HEAD_DIM
    ones = (seg[:, None] == seg[None, :]).astype(BF16)
    hmask = jnp.stack([jnp.broadcast_to(jnp.arange(LANES) < A_HEAD_DIM, (Q_BLOCK, LANES)),
                       jnp.broadcast_to(jnp.arange(LANES) >= A_HEAD_DIM, (Q_BLOCK, LANES))]
                      ).astype(BF16)
    expand = (jnp.arange(LANES)[:, None] == jnp.arange(D_MODEL)[None, :] // A_HEAD_DIM).astype(BF16)
    qkv_groups = _qkv_proj(xs, _row(a_norm[0]), a_w_qkv[0].astype(BF16), qg, kg, ones)
    outs, lses = [], []
    for grp, (_, dil) in enumerate(A_GROUPS):
        o, lse = _attention_group(qkv_groups[grp], grp, dil, _attn_bias(dil), hmask)
        outs.append(o)
        lses.append(lse)
    xs = _post_mixer(xs, (outs, lses, expand), a_w_o[0].astype(BF16), _row(mlp_norm[0]),
                     mlp_w1[0].astype(BF16), mlp_w2[0].astype(BF16), _row(ple_norm[0]),
                     ple_w_gate[0].astype(BF16), p[0, 0], ple_w_proj[0].astype(BF16))

    xz = _norm_matmul(xs, _row(b_norm[0]), b_w_up[0].astype(BF16))
    wg = jnp.zeros((3 * B_INNER, LANES), F32).at[:, :2 * B_HEADS].set(b_w_gate[0]).astype(BF16)
    bg = jnp.zeros((1, LANES), F32).at[0, :2 * B_HEADS].set(b_b_gate[0])
    q, k, v, xc, gates = _conv_qkv(xz, b_conv_w[0], _row(b_conv_b[0]), _block_diag_tiles(b_w_q[0]),
                                   _block_diag_tiles(b_w_k[0]), _block_diag_tiles(b_w_v[0]), wg, bg)
    tri = (jnp.arange(B_CHUNK)[:, None] >= jnp.arange(B_CHUNK)[None, :]).astype(F32)
    mix = _mlstm_scan(q, k, v, gates, xc, xz, tri, _row(b_h_gain[0]), _row(b_skip[0]))
    xs = _post_mixer(xs, mix, b_w_down[0].astype(BF16), _row(mlp_norm[1]),
                     mlp_w1[1].astype(BF16), mlp_w2[1].astype(BF16), _row(ple_norm[1]),
                     ple_w_gate[1].astype(BF16), p[1, 0], ple_w_proj[1].astype(BF16))
    return xs.reshape(bsz, seq, D_MODEL)
```

```python
import functools

import jax
import jax.numpy as jnp
from jax import lax
from jax.experimental import pallas as pl
from jax.experimental.pallas import tpu as pltpu

F32 = jnp.float32
BF16 = jnp.bfloat16

D_MODEL = 1024
A_HEADS = 16
A_HEAD_DIM = D_MODEL // A_HEADS
A_GROUPS = ((128, 1), (512, 4), (2048, 16))
A_N_GROUPS = len(A_GROUPS)
Q_BLOCK = 128
B_INNER = 2 * D_MODEL
B_HEADS = 4
B_HEAD_DIM = B_INNER // B_HEADS
B_CONV = 4
B_QKV_BLOCK = 4
SCAN_CHUNK = 256
D_FF = 4 * D_MODEL
PLE_DIM = 256
EPS = 1e-6
NEG_INF = -1e30

MXU_TILE = 256
LANES = 128
N_SLABS = D_MODEL // LANES
VMEM_LIMIT_BYTES = 60000 * 1024
ROW_TILE = 512

NT_DIMS = (((1,), (1,)), ((), ()))
TN_DIMS = (((0,), (0,)), ((), ()))


def _params(n_axes):
    return pltpu.CompilerParams(dimension_semantics=("arbitrary",) * n_axes,
                                vmem_limit_bytes=VMEM_LIMIT_BYTES)


def _resident(shape):
    return pl.BlockSpec(shape, lambda *_: (0,) * len(shape), pipeline_mode=pl.Buffered(1))


def _rms(x, g):
    ms = jnp.mean(x * x, axis=-1, keepdims=True)
    return x * lax.rsqrt(ms + EPS) * g


def _split_bf16(v):
    hi = v.astype(BF16)
    return hi, (v - hi.astype(F32)).astype(BF16)


def _qkv_kernel(x_ref, g_ref, w_ref, qg_ref, kg_ref, ones_ref, *rest):
    out_refs, hn_sc = rest[:A_N_GROUPS], rest[A_N_GROUPS]
    hn = _rms(x_ref[...], g_ref[...])
    tm = hn.shape[0]
    for j in range(N_SLABS):
        hn_sc[j] = hn[:, j * LANES:(j + 1) * LANES]
    for grp, (_, dil) in enumerate(A_GROUPS):
        o_ref = out_refs[grp]
        n = tm // dil
        if dil == 1:
            hg = hn.astype(BF16)
        else:
            hg = jnp.concatenate(
                [jnp.concatenate([hn_sc[j, pl.ds(r, n, stride=dil), :] for j in range(N_SLABS)],
                                 axis=1) for r in range(dil)], axis=0).astype(BF16)
        for kind in range(3):
            c = 3 * grp + kind
            y = jnp.dot(hg, w_ref[:, c * D_MODEL:(c + 1) * D_MODEL], preferred_element_type=F32)
            if kind < 2:
                hi, lo = _split_bf16(y * y)
                parts = []
                for t in range(D_MODEL // MXU_TILE):
                    ts = slice(t * MXU_TILE, (t + 1) * MXU_TILE)
                    parts.append(jnp.dot(hi[:, ts], ones_ref[...], preferred_element_type=F32)
                                 + jnp.dot(lo[:, ts], ones_ref[...], preferred_element_type=F32))
                ss = jnp.concatenate(parts, axis=1)
                gain = qg_ref[grp] if kind == 0 else kg_ref[grp]
                y = y * lax.rsqrt(ss * (1.0 / A_HEAD_DIM) + EPS) * gain
            yb = y.astype(BF16)
            ks = slice(kind * D_MODEL, (kind + 1) * D_MODEL)
            for r in range(dil):
                o_ref[r, :, ks] = yb[r * n:(r + 1) * n]


def _qkv_proj(x, g, w, qg, kg, ones):
    s = x.shape[0]
    out_specs, out_shape = [], []
    for _, dil in A_GROUPS:
        out_specs.append(pl.BlockSpec((dil, ROW_TILE // dil, 3 * D_MODEL), lambda i: (0, i, 0)))
        out_shape.append(jax.ShapeDtypeStruct((dil, s // dil, 3 * D_MODEL), BF16))
    return pl.pallas_call(
        _qkv_kernel,
        grid=(s // ROW_TILE,),
        in_specs=[pl.BlockSpec((ROW_TILE, D_MODEL), lambda i: (i, 0)),
                  _resident(g.shape), _resident(w.shape), _resident(qg.shape),
                  _resident(kg.shape), _resident(ones.shape)],
        out_specs=out_specs,
        out_shape=out_shape,
        scratch_shapes=[pltpu.VMEM((N_SLABS, ROW_TILE, LANES), F32)],
        compiler_params=_params(1),
        name="qkv_proj",
    )(x, g, w, qg, kg, ones)


def _attn_kernel(q_ref, kp_ref, kc_ref, vp_ref, vc_ref, bias_ref, hmask_ref, o_ref, lse_ref, *,
                 dil):
    sel = (pl.program_id(0) == 0).astype(jnp.int32)
    res = pl.program_id(1)
    rows = pl.ds(res, Q_BLOCK, stride=dil) if dil > 1 else slice(None)
    lane = lax.broadcasted_iota(jnp.int32, (Q_BLOCK, LANES), 1)
    lo = lane < A_HEAD_DIM
    lse_tile = jnp.zeros((Q_BLOCK, LANES), F32)
    for pair in range(A_HEADS // 2):
        cs = slice(pair * LANES, (pair + 1) * LANES)
        q2 = q_ref[:, cs]
        kcat = jnp.concatenate([kp_ref[:, cs], kc_ref[:, cs]], axis=0)
        vcat = jnp.concatenate([vp_ref[:, cs], vc_ref[:, cs]], axis=0)
        outs = []
        for hh in range(2):
            h = 2 * pair + hh
            qh = q2 * hmask_ref[hh]
            s = lax.dot_general(qh, kcat, NT_DIMS, preferred_element_type=F32)
            s = s + bias_ref[sel, h]
            m = jnp.max(s, axis=-1, keepdims=True)
            p = jnp.exp(s - m)
            l = jnp.sum(p, axis=-1, keepdims=True)
            outs.append(jnp.dot(p.astype(BF16), vcat, preferred_element_type=F32) / l)
            lse_tile = jnp.where(lane == h, m + jnp.log(l), lse_tile)
        o_ref[pair, rows, :] = jnp.where(lo, outs[0], outs[1])
    lse_ref[rows, :] = lse_tile


def _attn_bias(dil):
    slopes = jnp.asarray([2.0 ** (-8.0 * (h + 1) / A_HEADS) for h in range(A_HEADS)], F32)
    row = jnp.arange(Q_BLOCK)[:, None]
    col = jnp.arange(Q_BLOCK)[None, :]

    def table(steps, valid):
        dist = (steps * dil).astype(F32)
        return jnp.where(valid[None], -(slopes[:, None, None] * dist[None]), NEG_INF)

    prev = table(Q_BLOCK + row - col, col >= row)
    cur = table(row - col, col <= row)
    normal = jnp.concatenate([prev, cur], axis=-1)
    firstb = jnp.concatenate([jnp.full_like(prev, NEG_INF), cur], axis=-1)
    return jnp.stack([normal, firstb])


def _attention_group(qkv_g, grp, dil, bias, hmask):
    _, sd, _ = qkv_g.shape
    s = sd * dil
    nb = sd // Q_BLOCK

    def col_spec(off, prev):
        if prev:
            return pl.BlockSpec((None, Q_BLOCK, D_MODEL),
                                lambda b, r: (r, jnp.maximum(b - 1, 0), off))
        return pl.BlockSpec((None, Q_BLOCK, D_MODEL), lambda b, r: (r, b, off))

    return pl.pallas_call(
        functools.partial(_attn_kernel, dil=dil),
        grid=(nb, dil),
        in_specs=[col_spec(0, False), col_spec(1, True), col_spec(1, False),
                  col_spec(2, True), col_spec(2, False),
                  _resident(bias.shape), _resident(hmask.shape)],
        out_specs=[pl.BlockSpec((N_SLABS, Q_BLOCK * dil, LANES), lambda b, r: (0, b, 0)),
                   pl.BlockSpec((Q_BLOCK * dil, LANES), lambda b, r: (b, 0))],
        out_shape=[jax.ShapeDtypeStruct((N_SLABS, s, LANES), F32),
                   jax.ShapeDtypeStruct((s, LANES), F32)],
        compiler_params=_params(2),
        name=f"dilated_attn_g{grp}",
    )(qkv_g, qkv_g, qkv_g, qkv_g, qkv_g, bias, hmask)


def _norm_matmul_kernel(x_ref, g_ref, w_ref, o_ref):
    hn = _rms(x_ref[...], g_ref[...]).astype(BF16)
    n = w_ref.shape[1]
    for c in range(n // D_MODEL):
        cs = slice(c * D_MODEL, (c + 1) * D_MODEL)
        o_ref[:, cs] = jnp.dot(hn, w_ref[:, cs], preferred_element_type=F32)


def _norm_matmul(x, g, w):
    s = x.shape[0]
    n = w.shape[1]
    return pl.pallas_call(
        _norm_matmul_kernel,
        grid=(s // ROW_TILE,),
        in_specs=[pl.BlockSpec((ROW_TILE, D_MODEL), lambda i: (i, 0)),
                  _resident(g.shape), _resident(w.shape)],
        out_specs=pl.BlockSpec((ROW_TILE, n), lambda i: (i, 0)),
        out_shape=jax.ShapeDtypeStruct((s, n), F32),
        compiler_params=_params(1),
        name="norm_matmul",
    )(x, g, w)


def _merge_groups(o_refs, lse_refs, expand_ref):
    lses = [r[...] for r in lse_refs]
    top = functools.reduce(jnp.maximum, lses)
    es = [jnp.exp(v - top) for v in lses]
    tot = functools.reduce(jnp.add, es)
    merged = None
    for o_ref, e in zip(o_refs, es):
        hi, lo = _split_bf16(e / tot)
        wide = (jnp.dot(hi, expand_ref[...], preferred_element_type=F32)
                + jnp.dot(lo, expand_ref[...], preferred_element_type=F32))
        o = jnp.concatenate([o_ref[j] for j in range(N_SLABS)], axis=1)
        merged = wide * o if merged is None else merged + wide * o
    return merged.astype(BF16)


def _post_kernel(*refs, merge):
    x_ref = refs[0]
    if merge:
        mix = _merge_groups(refs[1:1 + A_N_GROUPS], refs[1 + A_N_GROUPS:1 + 2 * A_N_GROUPS],
                            refs[1 + 2 * A_N_GROUPS])
        refs = refs[2 + 2 * A_N_GROUPS:]
    else:
        mix = refs[1][...]
        refs = refs[2:]
    wo_ref, gm_ref, w1_ref, w2_ref, gp_ref, wg_ref, p_ref, wp_ref, o_ref = refs
    x = x_ref[...] + jnp.dot(mix, wo_ref[...], preferred_element_type=F32)
    hn = _rms(x, gm_ref[...]).astype(BF16)
    acc = jnp.zeros_like(x)
    for c in range(D_FF // D_MODEL):
        cs = slice(c * D_MODEL, (c + 1) * D_MODEL)
        a = jnp.dot(hn, w1_ref[:, cs], preferred_element_type=F32)
        a = jnp.square(jnp.maximum(a, 0.0)).astype(BF16)
        acc = acc + jnp.dot(a, w2_ref[cs, :], preferred_element_type=F32)
    x = x + acc
    gate = jax.nn.sigmoid(jnp.dot(_rms(x, gp_ref[...]).astype(BF16), wg_ref[...],
                                  preferred_element_type=F32))
    emb = jnp.dot(p_ref[...].astype(BF16), wp_ref[...], preferred_element_type=F32)
    o_ref[...] = x + gate * emb


def _post_mixer(x, mix, wo, gm, w1, w2, gp, wg, p, layer, wp):
    s = x.shape[0]
    row = lambda width: pl.BlockSpec((ROW_TILE, width), lambda i: (i, 0))
    merge = isinstance(mix, tuple)
    if merge:
        outs, lses, expand = mix
        mix_args = [*outs, *lses, expand]
        mix_specs = ([pl.BlockSpec((N_SLABS, ROW_TILE, LANES), lambda i: (0, i, 0))] * len(outs)
                     + [row(LANES)] * len(lses) + [_resident(expand.shape)])
    else:
        mix_args = [mix]
        mix_specs = [row(mix.shape[1])]
    return pl.pallas_call(
        functools.partial(_post_kernel, merge=merge),
        grid=(s // ROW_TILE,),
        in_specs=[row(D_MODEL), *mix_specs, _resident(wo.shape), _resident(gm.shape),
                  _resident(w1.shape), _resident(w2.shape), _resident(gp.shape),
                  _resident(wg.shape),
                  pl.BlockSpec((None, ROW_TILE, PLE_DIM), lambda i: (layer, i, 0)),
                  _resident(wp.shape)],
        out_specs=row(D_MODEL),
        out_shape=jax.ShapeDtypeStruct((s, D_MODEL), F32),
        compiler_params=_params(1),
        name="post_mixer",
    )(x, *mix_args, wo, gm, w1, w2, gp, wg, p, wp)


HALO = 8


def _conv_qkv_kernel(xm_ref, halo_ref, cw_ref, cb_ref, wq_ref, wk_ref, wv_ref, wg_ref, bg_ref,
                     q_ref, k_ref, v_ref, xc_ref, g_ref):
    first = pl.program_id(0) == 0
    hrow = lax.broadcasted_iota(jnp.int32, (HALO, MXU_TILE), 0)
    gates = jnp.broadcast_to(bg_ref[...], g_ref.shape)
    for t in range(B_INNER // MXU_TILE):
        ts = slice(t * MXU_TILE, (t + 1) * MXU_TILE)
        xm = xm_ref[:, ts]
        halo = jnp.where(first, 0.0, halo_ref[:, ts])
        y = cb_ref[:, ts] + cw_ref[B_CONV - 1:B_CONV, ts] * xm
        for j in range(1, B_CONV):
            r = pltpu.roll(xm, j, 0)
            hr = pltpu.roll(halo, j, 0)
            top = jnp.where(hrow < j, hr, r[:HALO])
            shifted = jnp.concatenate([top, r[HALO:]], axis=0)
            y = y + cw_ref[B_CONV - 1 - j:B_CONV - j, ts] * shifted
        xc = jax.nn.silu(y)
        xc_ref[:, ts] = xc
        xcb = xc.astype(BF16)
        qf = jnp.dot(xcb, wq_ref[t], preferred_element_type=F32)
        kf = jnp.dot(xcb, wk_ref[t], preferred_element_type=F32)
        vf = jnp.dot(xm.astype(BF16), wv_ref[t], preferred_element_type=F32)
        qb, kb, vb = qf.astype(BF16), kf.astype(BF16), vf.astype(BF16)
        q_ref[:, ts] = qb
        k_ref[:, ts] = (kf * B_HEAD_DIM ** -0.5).astype(BF16)
        v_ref[:, ts] = vb
        for part, val in enumerate((qb, kb, vb)):
            ws = slice(part * B_INNER + t * MXU_TILE, part * B_INNER + (t + 1) * MXU_TILE)
            gates = gates + jnp.dot(val, wg_ref[ws, :], preferred_element_type=F32)
    g_ref[...] = gates


def _conv_qkv(xz, cw, cb, wq, wk, wv, wg, bg):
    s = xz.shape[0]
    tiles_per_halo = ROW_TILE // HALO
    row = lambda: pl.BlockSpec((ROW_TILE, B_INNER), lambda i: (i, 0))
    return pl.pallas_call(
        _conv_qkv_kernel,
        grid=(s // ROW_TILE,),
        in_specs=[row(),
                  pl.BlockSpec((HALO, B_INNER),
                               lambda i: (jnp.maximum(i * tiles_per_halo - 1, 0), 0)),
                  _resident(cw.shape), _resident(cb.shape), _resident(wq.shape),
                  _resident(wk.shape), _resident(wv.shape), _resident(wg.shape),
                  _resident(bg.shape)],
        out_specs=[row(), row(), row(), row(),
                   pl.BlockSpec((ROW_TILE, LANES), lambda i: (i, 0))],
        out_shape=[jax.ShapeDtypeStruct((s, B_INNER), BF16)] * 3
        + [jax.ShapeDtypeStruct((s, B_INNER), F32), jax.ShapeDtypeStruct((s, LANES), F32)],
        compiler_params=_params(1),
        name="conv_qkv_gates",
    )(xz, xz, cw, cb, wq, wk, wv, wg, bg)


def _mlstm_kernel(q_ref, k_ref, v_ref, g_ref, xc_ref, z_ref, tri_ref, hg_ref, sk_ref, o_ref,
                  c_sc, cb_sc, n_sc, m_sc):
    @pl.when(pl.program_id(0) == 0)
    def _():
        c_sc[...] = jnp.zeros_like(c_sc)
        cb_sc[...] = jnp.zeros_like(cb_sc)
        n_sc[...] = jnp.zeros_like(n_sc)
        m_sc[...] = jnp.full_like(m_sc, NEG_INF)

    chunk = q_ref.shape[0]
    g_all = g_ref[...]
    lf_all = jax.nn.log_sigmoid(g_all)
    b_all = jnp.dot(tri_ref[...], lf_all, preferred_element_type=F32,
                    precision=lax.Precision.HIGHEST)
    b_t = b_all.T
    g_t = g_all.T
    row = lax.broadcasted_iota(jnp.int32, (chunk, chunk), 0)
    col = lax.broadcasted_iota(jnp.int32, (chunk, chunk), 1)
    causal = col <= row
    for h in range(B_HEADS):
        hs = slice(h * B_HEAD_DIM, (h + 1) * B_HEAD_DIM)
        i_col = g_all[:, h:h + 1]
        b_col = b_all[:, B_HEADS + h:B_HEADS + h + 1]
        i_row = g_t[h:h + 1, :]
        b_row = b_t[B_HEADS + h:B_HEADS + h + 1, :]
        m_prev = m_sc[h, 0:1, 0:1]
        dmat = jnp.where(causal, b_col - b_row + i_row, NEG_INF)
        inter = b_col + m_prev
        m_t = jnp.maximum(inter, jnp.max(dmat, axis=-1, keepdims=True))
        qh = q_ref[:, hs]
        kh = k_ref[:, hs]
        vh = v_ref[:, hs]
        s = lax.dot_general(qh, kh, NT_DIMS, preferred_element_type=F32) * jnp.exp(dmat - m_t)
        sc = jnp.exp(inter - m_t)
        num = (sc * jnp.dot(qh, cb_sc[h], preferred_element_type=F32)
               + jnp.dot(s.astype(BF16), vh, preferred_element_type=F32))
        n_old = n_sc[h]
        den = (sc * jnp.sum(qh.astype(F32) * n_old, axis=-1, keepdims=True)
               + jnp.sum(s, axis=-1, keepdims=True))
        hv = num / jnp.maximum(jnp.abs(den), jnp.exp(-m_t))
        b_last = b_col[chunk - 1:chunk, :]
        g_col = b_last - b_col + i_col
        m_new = jnp.maximum(b_last + m_prev, jnp.max(g_col, axis=0, keepdims=True))
        decay = jnp.exp(b_last + m_prev - m_new)
        wk = kh.astype(F32) * jnp.exp(g_col - m_new)
        wkb = wk.astype(BF16)
        for rb in range(B_HEAD_DIM // LANES):
            rs = slice(rb * LANES, (rb + 1) * LANES)
            c_new = decay * c_sc[h, rs, :] + lax.dot_general(wkb[:, rs], vh, TN_DIMS,
                                                             preferred_element_type=F32)
            c_sc[h, rs, :] = c_new
            cb_sc[h, rs, :] = c_new.astype(BF16)
        n_sc[h] = decay * n_old + jnp.sum(wk, axis=0, keepdims=True)
        m_sc[h] = jnp.broadcast_to(m_new, m_sc.shape[1:])
        hn = _rms(hv, hg_ref[:, hs])
        o_ref[:, hs] = ((hn + sk_ref[:, hs] * xc_ref[:, hs]) * jax.nn.silu(z_ref[:, hs])).astype(BF16)


def _mlstm_scan(q, k, v, gates, xc, xz, tri, hg, sk):
    s = q.shape[0]
    blk = lambda j=0: pl.BlockSpec((SCAN_CHUNK, B_INNER), lambda c: (c, j))
    return pl.pallas_call(
        _mlstm_kernel,
        grid=(s // SCAN_CHUNK,),
        in_specs=[blk(), blk(), blk(), pl.BlockSpec((SCAN_CHUNK, LANES), lambda c: (c, 0)),
                  blk(), blk(1), _resident(tri.shape), _resident(hg.shape), _resident(sk.shape)],
        out_specs=blk(),
        out_shape=jax.ShapeDtypeStruct((s, B_INNER), BF16),
        scratch_shapes=[pltpu.VMEM((B_HEADS, B_HEAD_DIM, B_HEAD_DIM), F32),
                        pltpu.VMEM((B_HEADS, B_HEAD_DIM, B_HEAD_DIM), BF16),
                        pltpu.VMEM((B_HEADS, 1, B_HEAD_DIM), F32),
                        pltpu.VMEM((B_HEADS, 8, LANES), F32)],
        compiler_params=_params(1),
        name="mlstm_scan",
    )(q, k, v, gates, xc, xz, tri, hg, sk)


def _block_diag_tiles(w):
    nblk, blk, _ = w.shape
    rows = w.reshape(nblk * blk // MXU_TILE, MXU_TILE, blk)
    idx = jnp.arange(MXU_TILE) // blk
    same_block = (idx[:, None] == idx[None, :]).astype(w.dtype)
    return (jnp.tile(rows, (1, 1, MXU_TILE // blk)) * same_block[None]).astype(BF16)


def _row(v):
    return v.reshape(1, -1).astype(F32)


def kernel(x, p, a_norm, a_w_qkv, a_q_gain, a_k_gain, a_w_o, b_norm, b_w_up, b_conv_w, b_conv_b,
           b_w_q, b_w_k, b_w_v, b_w_gate, b_b_gate, b_h_gain, b_skip, b_w_down, mlp_norm, mlp_w1,
           mlp_w2, ple_norm, ple_w_gate, ple_w_proj):
    bsz, seq, _ = x.shape
    assert bsz == 1
    xs = x.reshape(seq, D_MODEL)
    ps = p.reshape(p.shape[0], seq, PLE_DIM)

    qg = jnp.tile(a_q_gain[0], (1, A_HEADS)).reshape(A_N_GROUPS, 1, D_MODEL) * A_HEAD_DIM ** -0.5
    kg = jnp.tile(a_k_gain[0], (1, A_HEADS)).reshape(A_N_GROUPS, 1, D_MODEL)
    seg = jnp.arange(MXU_TILE) // A_HEAD_DIM
    ones = (seg[:, None] == seg[None, :]).astype(BF16)
    hmask = jnp.stack([jnp.broadcast_to(jnp.arange(LANES) < A_HEAD_DIM, (Q_BLOCK, LANES)),
                       jnp.broadcast_to(jnp.arange(LANES) >= A_HEAD_DIM, (Q_BLOCK, LANES))]
                      ).astype(BF16)
    expand = (jnp.arange(LANES)[:, None] == jnp.arange(D_MODEL)[None, :] // A_HEAD_DIM).astype(BF16)
    qkv_groups = _qkv_proj(xs, _row(a_norm[0]), a_w_qkv[0].astype(BF16), qg, kg, ones)
    outs, lses = [], []
    for grp, (_, dil) in enumerate(A_GROUPS):
        o, lse = _attention_group(qkv_groups[grp], grp, dil, _attn_bias(dil), hmask)
        outs.append(o)
        lses.append(lse)
    xs = _post_mixer(xs, (outs, lses, expand), a_w_o[0].astype(BF16), _row(mlp_norm[0]),
                     mlp_w1[0].astype(BF16), mlp_w2[0].astype(BF16), _row(ple_norm[0]),
                     ple_w_gate[0].astype(BF16), ps, 0, ple_w_proj[0].astype(BF16))

    xz = _norm_matmul(xs, _row(b_norm[0]), b_w_up[0].astype(BF16))
    wg = jnp.zeros((3 * B_INNER, LANES), F32).at[:, :2 * B_HEADS].set(b_w_gate[0]).astype(BF16)
    bg = jnp.zeros((1, LANES), F32).at[0, :2 * B_HEADS].set(b_b_gate[0])
    q, k, v, xc, gates = _conv_qkv(xz, b_conv_w[0], _row(b_conv_b[0]), _block_diag_tiles(b_w_q[0]),
                                   _block_diag_tiles(b_w_k[0]), _block_diag_tiles(b_w_v[0]), wg, bg)
    tri = (jnp.arange(SCAN_CHUNK)[:, None] >= jnp.arange(SCAN_CHUNK)[None, :]).astype(F32)
    mix = _mlstm_scan(q, k, v, gates, xc, xz, tri, _row(b_h_gain[0]), _row(b_skip[0]))
    xs = _post_mixer(xs, mix, b_w_down[0].astype(BF16), _row(mlp_norm[1]),
                     mlp_w1[1].astype(BF16), mlp_w2[1].astype(BF16), _row(ple_norm[1]),
                     ple_w_gate[1].astype(BF16), ps, 1, ple_w_proj[1].astype(BF16))
    return xs.reshape(bsz, seq, D_MODEL)
```

```python
import functools

import jax
import jax.numpy as jnp
from jax import lax
from jax.experimental import pallas as pl
from jax.experimental.pallas import tpu as pltpu

F32 = jnp.float32
BF16 = jnp.bfloat16

D_MODEL = 1024
A_HEADS = 16
A_HEAD_DIM = D_MODEL // A_HEADS
A_GROUPS = ((128, 1), (512, 4), (2048, 16))
A_N_GROUPS = len(A_GROUPS)
Q_BLOCK = 128
B_INNER = 2 * D_MODEL
B_HEADS = 4
B_HEAD_DIM = B_INNER // B_HEADS
B_CONV = 4
B_QKV_BLOCK = 4
SCAN_CHUNK = 256
D_FF = 4 * D_MODEL
PLE_DIM = 256
EPS = 1e-6
NEG_INF = -1e30

MXU_TILE = 256
LANES = 128
N_SLABS = D_MODEL // LANES
VMEM_LIMIT_BYTES = 60000 * 1024
ROW_TILE = 512

NT_DIMS = (((1,), (1,)), ((), ()))
TN_DIMS = (((0,), (0,)), ((), ()))


def _params(n_axes):
    return pltpu.CompilerParams(dimension_semantics=("arbitrary",) * n_axes,
                                vmem_limit_bytes=VMEM_LIMIT_BYTES)


def _resident(shape):
    return pl.BlockSpec(shape, lambda *_: (0,) * len(shape), pipeline_mode=pl.Buffered(1))


def _rms(x, g):
    ms = jnp.mean(x * x, axis=-1, keepdims=True)
    return x * lax.rsqrt(ms + EPS) * g


def _split_bf16(v):
    hi = v.astype(BF16)
    return hi, (v - hi.astype(F32)).astype(BF16)


def _qkv_kernel(x_ref, g_ref, w_ref, qg_ref, kg_ref, ones_ref, *rest):
    out_refs, hn_sc = rest[:A_N_GROUPS], rest[A_N_GROUPS]
    hn = _rms(x_ref[...], g_ref[...])
    tm = hn.shape[0]
    for j in range(N_SLABS):
        hn_sc[j] = hn[:, j * LANES:(j + 1) * LANES]
    for grp, (_, dil) in enumerate(A_GROUPS):
        o_ref = out_refs[grp]
        n = tm // dil
        if dil == 1:
            hg = hn.astype(BF16)
        else:
            hg = jnp.concatenate(
                [jnp.concatenate([hn_sc[j, pl.ds(r, n, stride=dil), :] for j in range(N_SLABS)],
                                 axis=1) for r in range(dil)], axis=0).astype(BF16)
        for kind in range(3):
            c = 3 * grp + kind
            y = jnp.dot(hg, w_ref[:, c * D_MODEL:(c + 1) * D_MODEL], preferred_element_type=F32)
            if kind < 2:
                y2 = (y * y).astype(BF16)
                parts = [jnp.dot(y2[:, t * MXU_TILE:(t + 1) * MXU_TILE], ones_ref[...],
                                 preferred_element_type=F32) for t in range(D_MODEL // MXU_TILE)]
                ss = jnp.concatenate(parts, axis=1)
                gain = qg_ref[grp] if kind == 0 else kg_ref[grp]
                y = y * lax.rsqrt(ss * (1.0 / A_HEAD_DIM) + EPS) * gain
            yb = y.astype(BF16)
            ks = slice(kind * D_MODEL, (kind + 1) * D_MODEL)
            for r in range(dil):
                o_ref[r, :, ks] = yb[r * n:(r + 1) * n]


def _qkv_proj(x, g, w, qg, kg, ones):
    s = x.shape[0]
    out_specs, out_shape = [], []
    for _, dil in A_GROUPS:
        out_specs.append(pl.BlockSpec((dil, ROW_TILE // dil, 3 * D_MODEL), lambda i: (0, i, 0)))
        out_shape.append(jax.ShapeDtypeStruct((dil, s // dil, 3 * D_MODEL), BF16))
    return pl.pallas_call(
        _qkv_kernel,
        grid=(s // ROW_TILE,),
        in_specs=[pl.BlockSpec((ROW_TILE, D_MODEL), lambda i: (i, 0)),
                  _resident(g.shape), _resident(w.shape), _resident(qg.shape),
                  _resident(kg.shape), _resident(ones.shape)],
        out_specs=out_specs,
        out_shape=out_shape,
        scratch_shapes=[pltpu.VMEM((N_SLABS, ROW_TILE, LANES), F32)],
        compiler_params=_params(1),
        name="qkv_proj",
    )(x, g, w, qg, kg, ones)


def _attn_kernel(q_ref, kp_ref, kc_ref, vp_ref, vc_ref, bias_ref, hmask_ref, o_ref, m_ref, l_ref, *,
                 dil):
    sel = (pl.program_id(0) == 0).astype(jnp.int32)
    res = pl.program_id(1)
    rows = pl.ds(res, Q_BLOCK, stride=dil) if dil > 1 else slice(None)
    lane = lax.broadcasted_iota(jnp.int32, (Q_BLOCK, LANES), 1)
    lo = lane < A_HEAD_DIM
    m_tile = jnp.zeros((Q_BLOCK, LANES), F32)
    l_tile = jnp.ones((Q_BLOCK, LANES), F32)
    for pair in range(A_HEADS // 2):
        cs = slice(pair * LANES, (pair + 1) * LANES)
        q2 = q_ref[:, cs]
        kcat = jnp.concatenate([kp_ref[:, cs], kc_ref[:, cs]], axis=0)
        vcat = jnp.concatenate([vp_ref[:, cs], vc_ref[:, cs]], axis=0)
        qq = jnp.concatenate([q2 * hmask_ref[0], q2 * hmask_ref[1]], axis=0)
        s = lax.dot_general(qq, kcat, NT_DIMS, preferred_element_type=F32)
        s = s + bias_ref[sel, pair]
        m = jnp.max(s, axis=-1, keepdims=True)
        p = jnp.exp(s - m)
        l = jnp.sum(p, axis=-1, keepdims=True)
        u = jnp.dot(p.astype(BF16), vcat, preferred_element_type=F32)
        for hh in range(2):
            hr = slice(hh * Q_BLOCK, (hh + 1) * Q_BLOCK)
            m_tile = jnp.where(lane == 2 * pair + hh, m[hr], m_tile)
            l_tile = jnp.where(lane == 2 * pair + hh, l[hr], l_tile)
        o_ref[pair, rows, :] = jnp.where(lo, u[:Q_BLOCK], u[Q_BLOCK:])
    m_ref[rows, :] = m_tile
    l_ref[rows, :] = l_tile


def _attn_bias(dil):
    slopes = jnp.asarray([2.0 ** (-8.0 * (h + 1) / A_HEADS) for h in range(A_HEADS)], F32)
    row = jnp.arange(Q_BLOCK)[:, None]
    col = jnp.arange(Q_BLOCK)[None, :]

    def table(steps, valid):
        dist = (steps * dil).astype(F32)
        return jnp.where(valid[None], -(slopes[:, None, None] * dist[None]), NEG_INF)

    prev = table(Q_BLOCK + row - col, col >= row)
    cur = table(row - col, col <= row)
    normal = jnp.concatenate([prev, cur], axis=-1)
    firstb = jnp.concatenate([jnp.full_like(prev, NEG_INF), cur], axis=-1)
    return jnp.stack([normal, firstb]).reshape(2, A_HEADS // 2, 2 * Q_BLOCK, 2 * Q_BLOCK)


def _attention_group(qkv_g, grp, dil, bias, hmask):
    _, sd, _ = qkv_g.shape
    s = sd * dil
    nb = sd // Q_BLOCK

    def col_spec(off, prev):
        if prev:
            return pl.BlockSpec((None, Q_BLOCK, D_MODEL),
                                lambda b, r: (r, jnp.maximum(b - 1, 0), off))
        return pl.BlockSpec((None, Q_BLOCK, D_MODEL), lambda b, r: (r, b, off))

    return pl.pallas_call(
        functools.partial(_attn_kernel, dil=dil),
        grid=(nb, dil),
        in_specs=[col_spec(0, False), col_spec(1, True), col_spec(1, False),
                  col_spec(2, True), col_spec(2, False),
                  _resident(bias.shape), _resident(hmask.shape)],
        out_specs=[pl.BlockSpec((N_SLABS, Q_BLOCK * dil, LANES), lambda b, r: (0, b, 0)),
                   pl.BlockSpec((Q_BLOCK * dil, LANES), lambda b, r: (b, 0)),
                   pl.BlockSpec((Q_BLOCK * dil, LANES), lambda b, r: (b, 0))],
        out_shape=[jax.ShapeDtypeStruct((N_SLABS, s, LANES), F32),
                   jax.ShapeDtypeStruct((s, LANES), F32), jax.ShapeDtypeStruct((s, LANES), F32)],
        compiler_params=_params(2),
        name=f"dilated_attn_g{grp}",
    )(qkv_g, qkv_g, qkv_g, qkv_g, qkv_g, bias, hmask)


def _merge_groups(u_refs, m_refs, l_refs, expand_ref):
    ms = [r[...] for r in m_refs]
    top = functools.reduce(jnp.maximum, ms)
    es = [jnp.exp(v - top) for v in ms]
    den = functools.reduce(jnp.add, [e * r[...] for e, r in zip(es, l_refs)])
    merged = None
    for u_ref, e in zip(u_refs, es):
        hi, lo = _split_bf16(e / den)
        wide = (jnp.dot(hi, expand_ref[...], preferred_element_type=F32)
                + jnp.dot(lo, expand_ref[...], preferred_element_type=F32))
        u = jnp.concatenate([u_ref[j] for j in range(N_SLABS)], axis=1)
        merged = wide * u if merged is None else merged + wide * u
    return merged.astype(BF16)


def _post_kernel(*refs, merge):
    x_ref = refs[0]
    if merge:
        g = A_N_GROUPS
        mix = _merge_groups(refs[1:1 + g], refs[1 + g:1 + 2 * g], refs[1 + 2 * g:1 + 3 * g],
                            refs[1 + 3 * g])
        refs = refs[2 + 3 * g:]
    else:
        mix = refs[1][...]
        refs = refs[2:]
    wo_ref, gm_ref, w1_ref, w2_ref, gp_ref, wg_ref, p_ref, wp_ref, o_ref = refs
    x = x_ref[...] + jnp.dot(mix, wo_ref[...], preferred_element_type=F32)
    hn = _rms(x, gm_ref[...]).astype(BF16)
    acc = jnp.zeros_like(x)
    for c in range(D_FF // D_MODEL):
        cs = slice(c * D_MODEL, (c + 1) * D_MODEL)
        a = jnp.dot(hn, w1_ref[:, cs], preferred_element_type=F32)
        a = jnp.square(jnp.maximum(a, 0.0)).astype(BF16)
        acc = acc + jnp.dot(a, w2_ref[cs, :], preferred_element_type=F32)
    x = x + acc
    gate = jax.nn.sigmoid(jnp.dot(_rms(x, gp_ref[...]).astype(BF16), wg_ref[...],
                                  preferred_element_type=F32))
    emb = jnp.dot(p_ref[...].astype(BF16), wp_ref[...], preferred_element_type=F32)
    o_ref[...] = x + gate * emb


def _post_mixer(x, mix, wo, gm, w1, w2, gp, wg, p, layer, wp):
    s = x.shape[0]
    row = lambda width: pl.BlockSpec((ROW_TILE, width), lambda i: (i, 0))
    merge = isinstance(mix, tuple)
    if merge:
        sums, maxs, dens, expand = mix
        mix_args = [*sums, *maxs, *dens, expand]
        mix_specs = ([pl.BlockSpec((N_SLABS, ROW_TILE, LANES), lambda i: (0, i, 0))] * len(sums)
                     + [row(LANES)] * (len(maxs) + len(dens)) + [_resident(expand.shape)])
    else:
        mix_args = [mix]
        mix_specs = [row(mix.shape[1])]
    return pl.pallas_call(
        functools.partial(_post_kernel, merge=merge),
        grid=(s // ROW_TILE,),
        in_specs=[row(D_MODEL), *mix_specs, _resident(wo.shape), _resident(gm.shape),
                  _resident(w1.shape), _resident(w2.shape), _resident(gp.shape),
                  _resident(wg.shape),
                  pl.BlockSpec((None, ROW_TILE, PLE_DIM), lambda i: (layer, i, 0)),
                  _resident(wp.shape)],
        out_specs=row(D_MODEL),
        out_shape=jax.ShapeDtypeStruct((s, D_MODEL), F32),
        compiler_params=_params(1),
        name="post_mixer",
    )(x, *mix_args, wo, gm, w1, w2, gp, wg, p, wp)


HALO = 8


def _up_conv_kernel(x_ref, g_ref, wup_ref, cw_ref, cb_ref, wq_ref, wk_ref, wv_ref, wg_ref, bg_ref,
                    q_ref, k_ref, v_ref, xc_ref, z_ref, gt_ref, halo_sc):
    @pl.when(pl.program_id(0) == 0)
    def _():
        halo_sc[...] = jnp.zeros_like(halo_sc)

    hn = _rms(x_ref[...], g_ref[...]).astype(BF16)
    tm = hn.shape[0]
    hrow = lax.broadcasted_iota(jnp.int32, (HALO, MXU_TILE), 0)
    gates = jnp.broadcast_to(bg_ref[...], gt_ref.shape)
    for t in range(B_INNER // MXU_TILE):
        ts = slice(t * MXU_TILE, (t + 1) * MXU_TILE)
        xm = jnp.dot(hn, wup_ref[:, ts], preferred_element_type=F32)
        halo = halo_sc[:, ts]
        halo_sc[:, ts] = xm[tm - HALO:]
        y = cb_ref[:, ts] + cw_ref[B_CONV - 1:B_CONV, ts] * xm
        for j in range(1, B_CONV):
            r = pltpu.roll(xm, j, 0)
            hr = pltpu.roll(halo, j, 0)
            top = jnp.where(hrow < j, hr, r[:HALO])
            shifted = jnp.concatenate([top, r[HALO:]], axis=0)
            y = y + cw_ref[B_CONV - 1 - j:B_CONV - j, ts] * shifted
        xc = jax.nn.silu(y)
        xc_ref[:, ts] = xc
        xcb = xc.astype(BF16)
        qf = jnp.dot(xcb, wq_ref[t], preferred_element_type=F32)
        kf = jnp.dot(xcb, wk_ref[t], preferred_element_type=F32)
        vf = jnp.dot(xm.astype(BF16), wv_ref[t], preferred_element_type=F32)
        qb, kb, vb = qf.astype(BF16), kf.astype(BF16), vf.astype(BF16)
        q_ref[:, ts] = qb
        k_ref[:, ts] = (kf * B_HEAD_DIM ** -0.5).astype(BF16)
        v_ref[:, ts] = vb
        for part, val in enumerate((qb, kb, vb)):
            ws = slice(part * B_INNER + t * MXU_TILE, part * B_INNER + (t + 1) * MXU_TILE)
            gates = gates + jnp.dot(val, wg_ref[ws, :], preferred_element_type=F32)
    gt_ref[...] = gates
    for c in range(B_INNER // D_MODEL):
        cs = slice(c * D_MODEL, (c + 1) * D_MODEL)
        z_ref[:, cs] = jnp.dot(hn, wup_ref[:, B_INNER + c * D_MODEL:B_INNER + (c + 1) * D_MODEL],
                               preferred_element_type=F32)


def _up_conv(x, g, wup, cw, cb, wq, wk, wv, wg, bg):
    s = x.shape[0]
    row = lambda: pl.BlockSpec((ROW_TILE, B_INNER), lambda i: (i, 0))
    return pl.pallas_call(
        _up_conv_kernel,
        grid=(s // ROW_TILE,),
        in_specs=[pl.BlockSpec((ROW_TILE, D_MODEL), lambda i: (i, 0)),
                  _resident(g.shape), _resident(wup.shape),
                  _resident(cw.shape), _resident(cb.shape), _resident(wq.shape),
                  _resident(wk.shape), _resident(wv.shape), _resident(wg.shape),
                  _resident(bg.shape)],
        out_specs=[row(), row(), row(), row(), row(),
                   pl.BlockSpec((ROW_TILE, LANES), lambda i: (i, 0))],
        out_shape=[jax.ShapeDtypeStruct((s, B_INNER), BF16)] * 3
        + [jax.ShapeDtypeStruct((s, B_INNER), F32)] * 2 + [jax.ShapeDtypeStruct((s, LANES), F32)],
        scratch_shapes=[pltpu.VMEM((HALO, B_INNER), F32)],
        compiler_params=_params(1),
        name="up_conv_qkv_gates",
    )(x, g, wup, cw, cb, wq, wk, wv, wg, bg)


def _mlstm_kernel(q_ref, k_ref, v_ref, g_ref, xc_ref, z_ref, tri_ref, hg_ref, sk_ref, o_ref,
                  c_sc, cb_sc, n_sc, m_sc):
    @pl.when(pl.program_id(0) == 0)
    def _():
        c_sc[...] = jnp.zeros_like(c_sc)
        cb_sc[...] = jnp.zeros_like(cb_sc)
        n_sc[...] = jnp.zeros_like(n_sc)
        m_sc[...] = jnp.full_like(m_sc, NEG_INF)

    chunk = q_ref.shape[0]
    g_all = g_ref[...]
    lf_all = jax.nn.log_sigmoid(g_all)
    b_all = jnp.dot(tri_ref[...], lf_all, preferred_element_type=F32,
                    precision=lax.Precision.HIGHEST)
    b_t = b_all.T
    g_t = g_all.T
    row = lax.broadcasted_iota(jnp.int32, (chunk, chunk), 0)
    col = lax.broadcasted_iota(jnp.int32, (chunk, chunk), 1)
    causal = col <= row
    for h in range(B_HEADS):
        hs = slice(h * B_HEAD_DIM, (h + 1) * B_HEAD_DIM)
        i_col = g_all[:, h:h + 1]
        b_col = b_all[:, B_HEADS + h:B_HEADS + h + 1]
        i_row = g_t[h:h + 1, :]
        b_row = b_t[B_HEADS + h:B_HEADS + h + 1, :]
        m_prev = m_sc[h, 0:1, 0:1]
        dmat = jnp.where(causal, b_col - b_row + i_row, NEG_INF)
        inter = b_col + m_prev
        m_t = jnp.maximum(inter, jnp.max(dmat, axis=-1, keepdims=True))
        qh = q_ref[:, hs]
        kh = k_ref[:, hs]
        vh = v_ref[:, hs]
        s = lax.dot_general(qh, kh, NT_DIMS, preferred_element_type=F32) * jnp.exp(dmat - m_t)
        sc = jnp.exp(inter - m_t)
        num = (sc * jnp.dot(qh, cb_sc[h], preferred_element_type=F32)
               + jnp.dot(s.astype(BF16), vh, preferred_element_type=F32))
        n_old = n_sc[h]
        den = (sc * jnp.sum(qh.astype(F32) * n_old, axis=-1, keepdims=True)
               + jnp.sum(s, axis=-1, keepdims=True))
        hv = num / jnp.maximum(jnp.abs(den), jnp.exp(-m_t))
        b_last = b_col[chunk - 1:chunk, :]
        g_col = b_last - b_col + i_col
        m_new = jnp.maximum(b_last + m_prev, jnp.max(g_col, axis=0, keepdims=True))
        decay = jnp.exp(b_last + m_prev - m_new)
        wk = kh.astype(F32) * jnp.exp(g_col - m_new)
        wkb = wk.astype(BF16)
        for rb in range(B_HEAD_DIM // LANES):
            rs = slice(rb * LANES, (rb + 1) * LANES)
            c_new = decay * c_sc[h, rs, :] + lax.dot_general(wkb[:, rs], vh, TN_DIMS,
                                                             preferred_element_type=F32)
            c_sc[h, rs, :] = c_new
            cb_sc[h, rs, :] = c_new.astype(BF16)
        n_sc[h] = decay * n_old + jnp.sum(wk, axis=0, keepdims=True)
        m_sc[h] = jnp.broadcast_to(m_new, m_sc.shape[1:])
        hn = _rms(hv, hg_ref[:, hs])
        o_ref[:, hs] = ((hn + sk_ref[:, hs] * xc_ref[:, hs]) * jax.nn.silu(z_ref[:, hs])).astype(BF16)


def _mlstm_scan(q, k, v, gates, xc, z, tri, hg, sk):
    s = q.shape[0]
    blk = lambda: pl.BlockSpec((SCAN_CHUNK, B_INNER), lambda c: (c, 0))
    return pl.pallas_call(
        _mlstm_kernel,
        grid=(s // SCAN_CHUNK,),
        in_specs=[blk(), blk(), blk(), pl.BlockSpec((SCAN_CHUNK, LANES), lambda c: (c, 0)),
                  blk(), blk(), _resident(tri.shape), _resident(hg.shape), _resident(sk.shape)],
        out_specs=blk(),
        out_shape=jax.ShapeDtypeStruct((s, B_INNER), BF16),
        scratch_shapes=[pltpu.VMEM((B_HEADS, B_HEAD_DIM, B_HEAD_DIM), F32),
                        pltpu.VMEM((B_HEADS, B_HEAD_DIM, B_HEAD_DIM), BF16),
                        pltpu.VMEM((B_HEADS, 1, B_HEAD_DIM), F32),
                        pltpu.VMEM((B_HEADS, 8, LANES), F32)],
        compiler_params=_params(1),
        name="mlstm_scan",
    )(q, k, v, gates, xc, z, tri, hg, sk)


def _block_diag_tiles(w):
    nblk, blk, _ = w.shape
    rows = w.reshape(nblk * blk // MXU_TILE, MXU_TILE, blk)
    idx = jnp.arange(MXU_TILE) // blk
    same_block = (idx[:, None] == idx[None, :]).astype(w.dtype)
    return (jnp.tile(rows, (1, 1, MXU_TILE // blk)) * same_block[None]).astype(BF16)


def _row(v):
    return v.reshape(1, -1).astype(F32)


def kernel(x, p, a_norm, a_w_qkv, a_q_gain, a_k_gain, a_w_o, b_norm, b_w_up, b_conv_w, b_conv_b,
           b_w_q, b_w_k, b_w_v, b_w_gate, b_b_gate, b_h_gain, b_skip, b_w_down, mlp_norm, mlp_w1,
           mlp_w2, ple_norm, ple_w_gate, ple_w_proj):
    bsz, seq, _ = x.shape
    assert bsz == 1
    xs = x.reshape(seq, D_MODEL)
    ps = p.reshape(p.shape[0], seq, PLE_DIM)

    qg = jnp.tile(a_q_gain[0], (1, A_HEADS)).reshape(A_N_GROUPS, 1, D_MODEL) * A_HEAD_DIM ** -0.5
    kg = jnp.tile(a_k_gain[0], (1, A_HEADS)).reshape(A_N_GROUPS, 1, D_MODEL)
    seg = jnp.arange(MXU_TILE) // A_HEAD_DIM
    ones = (seg[:, None] == seg[None, :]).astype(BF16)
    hmask = jnp.stack([jnp.broadcast_to(jnp.arange(LANES) < A_HEAD_DIM, (Q_BLOCK, LANES)),
                       jnp.broadcast_to(jnp.arange(LANES) >= A_HEAD_DIM, (Q_BLOCK, LANES))]
                      ).astype(BF16)
    expand = (jnp.arange(LANES)[:, None] == jnp.arange(D_MODEL)[None, :] // A_HEAD_DIM).astype(BF16)
    qkv_groups = _qkv_proj(xs, _row(a_norm[0]), a_w_qkv[0].astype(BF16), qg, kg, ones)
    stats = [_attention_group(qkv_groups[grp], grp, dil, _attn_bias(dil), hmask)
             for grp, (_, dil) in enumerate(A_GROUPS)]
    sums, maxs, dens = zip(*stats)
    xs = _post_mixer(xs, (sums, maxs, dens, expand), a_w_o[0].astype(BF16), _row(mlp_norm[0]),
                     mlp_w1[0].astype(BF16), mlp_w2[0].astype(BF16), _row(ple_norm[0]),
                     ple_w_gate[0].astype(BF16), ps, 0, ple_w_proj[0].astype(BF16))

    wg = jnp.zeros((3 * B_INNER, LANES), F32).at[:, :2 * B_HEADS].set(b_w_gate[0]).astype(BF16)
    bg = jnp.zeros((1, LANES), F32).at[0, :2 * B_HEADS].set(b_b_gate[0])
    q, k, v, xc, z, gates = _up_conv(xs, _row(b_norm[0]), b_w_up[0].astype(BF16), b_conv_w[0],
                                     _row(b_conv_b[0]), _block_diag_tiles(b_w_q[0]),
                                     _block_diag_tiles(b_w_k[0]), _block_diag_tiles(b_w_v[0]), wg, bg)
    tri = (jnp.arange(SCAN_CHUNK)[:, None] >= jnp.arange(SCAN_CHUNK)[None, :]).astype(F32)
    mix = _mlstm_scan(q, k, v, gates, xc, z, tri, _row(b_h_gain[0]), _row(b_skip[0]))
    xs = _post_mixer(xs, mix, b_w_down[0].astype(BF16), _row(mlp_norm[1]),
                     mlp_w1[1].astype(BF16), mlp_w2[1].astype(BF16), _row(ple_norm[1]),
                     ple_w_gate[1].astype(BF16), ps, 1, ple_w_proj[1].astype(BF16))
    return xs.reshape(bsz, seq, D_MODEL)
```

```python
import functools

import jax
import jax.numpy as jnp
import numpy as np
from jax import lax
from jax.experimental import pallas as pl
from jax.experimental.pallas import tpu as pltpu

F32 = jnp.float32
BF16 = jnp.bfloat16

D_MODEL = 1024
A_HEADS = 16
A_HEAD_DIM = D_MODEL // A_HEADS
A_GROUPS = ((128, 1), (512, 4), (2048, 16))
A_N_GROUPS = len(A_GROUPS)
Q_BLOCK = 128
TILES_PER_STEP = 8
B_INNER = 2 * D_MODEL
B_HEADS = 4
B_HEAD_DIM = B_INNER // B_HEADS
B_CONV = 4
B_QKV_BLOCK = 4
SCAN_CHUNK = 256
D_FF = 4 * D_MODEL
PLE_DIM = 256
EPS = 1e-6
NEG_INF = -1e30

MXU_TILE = 256
LANES = 128
N_SLABS = D_MODEL // LANES
VMEM_LIMIT_BYTES = 60000 * 1024
ROW_TILE = 512

NT_DIMS = (((1,), (1,)), ((), ()))
TN_DIMS = (((0,), (0,)), ((), ()))


def _params(n_axes):
    return pltpu.CompilerParams(dimension_semantics=("arbitrary",) * n_axes,
                                vmem_limit_bytes=VMEM_LIMIT_BYTES)


def _resident(shape):
    return pl.BlockSpec(shape, lambda *_: (0,) * len(shape), pipeline_mode=pl.Buffered(1))


def _rms(x, g):
    ms = jnp.mean(x * x, axis=-1, keepdims=True)
    return x * lax.rsqrt(ms + EPS) * g


def _split_bf16(v):
    hi = v.astype(BF16)
    return hi, (v - hi.astype(F32)).astype(BF16)


def _qkv_kernel(x_ref, g_ref, w_ref, qg_ref, kg_ref, ones_ref, *rest):
    out_refs, hn_sc = rest[:A_N_GROUPS], rest[A_N_GROUPS]
    hn = _rms(x_ref[...], g_ref[...])
    tm = hn.shape[0]
    for j in range(N_SLABS):
        hn_sc[j] = hn[:, j * LANES:(j + 1) * LANES]
    for grp, (_, dil) in enumerate(A_GROUPS):
        o_ref = out_refs[grp]
        n = tm // dil
        if dil == 1:
            hg = hn.astype(BF16)
        else:
            hg = jnp.concatenate(
                [jnp.concatenate([hn_sc[j, pl.ds(r, n, stride=dil), :] for j in range(N_SLABS)],
                                 axis=1) for r in range(dil)], axis=0).astype(BF16)
        for kind in range(3):
            c = 3 * grp + kind
            y = jnp.dot(hg, w_ref[:, c * D_MODEL:(c + 1) * D_MODEL], preferred_element_type=F32)
            if kind < 2:
                y2 = (y * y).astype(BF16)
                parts = [jnp.dot(y2[:, t * MXU_TILE:(t + 1) * MXU_TILE], ones_ref[...],
                                 preferred_element_type=F32) for t in range(D_MODEL // MXU_TILE)]
                ss = jnp.concatenate(parts, axis=1)
                gain = qg_ref[grp] if kind == 0 else kg_ref[grp]
                y = y * lax.rsqrt(ss * (1.0 / A_HEAD_DIM) + EPS) * gain
            yb = y.astype(BF16)
            ks = slice(kind * D_MODEL, (kind + 1) * D_MODEL)
            for r in range(dil):
                o_ref[r, :, ks] = yb[r * n:(r + 1) * n]


def _qkv_proj(x, g, w, qg, kg, ones):
    s = x.shape[0]
    out_specs, out_shape = [], []
    for _, dil in A_GROUPS:
        out_specs.append(pl.BlockSpec((dil, ROW_TILE // dil, 3 * D_MODEL), lambda i: (0, i, 0)))
        out_shape.append(jax.ShapeDtypeStruct((dil, s // dil, 3 * D_MODEL), BF16))
    return pl.pallas_call(
        _qkv_kernel,
        grid=(s // ROW_TILE,),
        in_specs=[pl.BlockSpec((ROW_TILE, D_MODEL), lambda i: (i, 0)),
                  _resident(g.shape), _resident(w.shape), _resident(qg.shape),
                  _resident(kg.shape), _resident(ones.shape)],
        out_specs=out_specs,
        out_shape=out_shape,
        scratch_shapes=[pltpu.VMEM((N_SLABS, ROW_TILE, LANES), F32)],
        compiler_params=_params(1),
        name="qkv_proj",
    )(x, g, w, qg, kg, ones)


def _attn_kernel(q_ref, kp_ref, kc_ref, vp_ref, vc_ref, bias_ref, hmask_ref, o_ref, m_ref, l_ref, *,
                 dil):
    n_res = q_ref.shape[0]
    n_tiles = q_ref.shape[1] // Q_BLOCK
    lane = lax.broadcasted_iota(jnp.int32, (Q_BLOCK, LANES), 1)
    lo = lane < A_HEAD_DIM
    for sub in range(n_res):
        res = pl.program_id(1) * n_res + sub
        for tile in range(n_tiles):
            cur = slice(tile * Q_BLOCK, (tile + 1) * Q_BLOCK)
            both = slice((tile - 1) * Q_BLOCK, (tile + 1) * Q_BLOCK)
            if dil > 1:
                rows = pl.ds(tile * Q_BLOCK * dil + res, Q_BLOCK, stride=dil)
            else:
                rows = cur
            if tile == 0:
                sel = (pl.program_id(0) == 0).astype(jnp.int32)
            else:
                sel = 0
            m_tile = jnp.zeros((Q_BLOCK, LANES), F32)
            l_tile = jnp.ones((Q_BLOCK, LANES), F32)
            for pair in range(A_HEADS // 2):
                cs = slice(pair * LANES, (pair + 1) * LANES)
                q2 = q_ref[sub, cur, cs]
                if tile == 0:
                    kcat = jnp.concatenate([kp_ref[sub, :, cs], kc_ref[sub, cur, cs]], axis=0)
                    vcat = jnp.concatenate([vp_ref[sub, :, cs], vc_ref[sub, cur, cs]], axis=0)
                else:
                    kcat = kc_ref[sub, both, cs]
                    vcat = vc_ref[sub, both, cs]
                qq = jnp.concatenate([q2 * hmask_ref[0], q2 * hmask_ref[1]], axis=0)
                s = lax.dot_general(qq, kcat, NT_DIMS, preferred_element_type=F32)
                s = s + bias_ref[sel, pair]
                m = jnp.max(s, axis=-1, keepdims=True)
                p = jnp.exp(s - m)
                l = jnp.sum(p, axis=-1, keepdims=True)
                u = jnp.dot(p.astype(BF16), vcat, preferred_element_type=F32)
                for hh in range(2):
                    hr = slice(hh * Q_BLOCK, (hh + 1) * Q_BLOCK)
                    m_tile = jnp.where(lane == 2 * pair + hh, m[hr], m_tile)
                    l_tile = jnp.where(lane == 2 * pair + hh, l[hr], l_tile)
                o_ref[pair, rows, :] = jnp.where(lo, u[:Q_BLOCK], u[Q_BLOCK:])
            m_ref[rows, :] = m_tile
            l_ref[rows, :] = l_tile


def _attn_bias(dil):
    slopes = np.asarray([2.0 ** (-8.0 * (h + 1) / A_HEADS) for h in range(A_HEADS)], np.float32)
    row = np.arange(Q_BLOCK)[:, None]
    col = np.arange(Q_BLOCK)[None, :]

    def table(steps, valid):
        dist = (steps * dil).astype(np.float32)
        return np.where(valid[None], -(slopes[:, None, None] * dist[None]), np.float32(NEG_INF))

    prev = table(Q_BLOCK + row - col, col >= row)
    cur = table(row - col, col <= row)
    normal = np.concatenate([prev, cur], axis=-1)
    firstb = np.concatenate([np.full_like(prev, NEG_INF), cur], axis=-1)
    tables = np.stack([normal, firstb]).astype(np.float32)
    return tables.reshape(2, A_HEADS // 2, 2 * Q_BLOCK, 2 * Q_BLOCK)


def _attention_group(qkv_g, grp, dil, bias, hmask):
    _, sd, _ = qkv_g.shape
    s = sd * dil
    n_res = min(TILES_PER_STEP, dil)
    n_tiles = TILES_PER_STEP // n_res
    blk = n_tiles * Q_BLOCK

    def col_spec(off, prev):
        if prev:
            return pl.BlockSpec((n_res, Q_BLOCK, D_MODEL),
                                lambda b, r: (r, jnp.maximum(b * n_tiles - 1, 0), off))
        return pl.BlockSpec((n_res, blk, D_MODEL), lambda b, r: (r, b, off))

    return pl.pallas_call(
        functools.partial(_attn_kernel, dil=dil),
        grid=(sd // blk, dil // n_res),
        in_specs=[col_spec(0, False), col_spec(1, True), col_spec(1, False),
                  col_spec(2, True), col_spec(2, False),
                  _resident(bias.shape), _resident(hmask.shape)],
        out_specs=[pl.BlockSpec((N_SLABS, blk * dil, LANES), lambda b, r: (0, b, 0)),
                   pl.BlockSpec((blk * dil, LANES), lambda b, r: (b, 0)),
                   pl.BlockSpec((blk * dil, LANES), lambda b, r: (b, 0))],
        out_shape=[jax.ShapeDtypeStruct((N_SLABS, s, LANES), F32),
                   jax.ShapeDtypeStruct((s, LANES), F32), jax.ShapeDtypeStruct((s, LANES), F32)],
        compiler_params=_params(2),
        name=f"dilated_attn_g{grp}",
    )(qkv_g, qkv_g, qkv_g, qkv_g, qkv_g, bias, hmask)


def _merge_groups(u_refs, m_refs, l_refs, expand_ref):
    ms = [r[...] for r in m_refs]
    top = functools.reduce(jnp.maximum, ms)
    es = [jnp.exp(v - top) for v in ms]
    den = functools.reduce(jnp.add, [e * r[...] for e, r in zip(es, l_refs)])
    merged = None
    for u_ref, e in zip(u_refs, es):
        hi, lo = _split_bf16(e / den)
        wide = (jnp.dot(hi, expand_ref[...], preferred_element_type=F32)
                + jnp.dot(lo, expand_ref[...], preferred_element_type=F32))
        u = jnp.concatenate([u_ref[j] for j in range(N_SLABS)], axis=1)
        merged = wide * u if merged is None else merged + wide * u
    return merged.astype(BF16)


def _post_kernel(*refs, merge):
    x_ref = refs[0]
    if merge:
        g = A_N_GROUPS
        mix = _merge_groups(refs[1:1 + g], refs[1 + g:1 + 2 * g], refs[1 + 2 * g:1 + 3 * g],
                            refs[1 + 3 * g])
        refs = refs[2 + 3 * g:]
    else:
        mix = refs[1][...]
        refs = refs[2:]
    wo_ref, gm_ref, w1_ref, w2_ref, gp_ref, wg_ref, p_ref, wp_ref, o_ref = refs
    x = x_ref[...] + jnp.dot(mix, wo_ref[...], preferred_element_type=F32)
    hn = _rms(x, gm_ref[...]).astype(BF16)
    acc = jnp.zeros_like(x)
    for c in range(D_FF // D_MODEL):
        cs = slice(c * D_MODEL, (c + 1) * D_MODEL)
        a = jnp.dot(hn, w1_ref[:, cs], preferred_element_type=F32)
        a = jnp.square(jnp.maximum(a, 0.0)).astype(BF16)
        acc = acc + jnp.dot(a, w2_ref[cs, :], preferred_element_type=F32)
    x = x + acc
    gate = jax.nn.sigmoid(jnp.dot(_rms(x, gp_ref[...]).astype(BF16), wg_ref[...],
                                  preferred_element_type=F32))
    emb = jnp.dot(p_ref[...].astype(BF16), wp_ref[...], preferred_element_type=F32)
    o_ref[...] = x + gate * emb


def _post_mixer(x, mix, wo, gm, w1, w2, gp, wg, p, layer, wp):
    s = x.shape[0]
    row = lambda width: pl.BlockSpec((ROW_TILE, width), lambda i: (i, 0))
    merge = isinstance(mix, tuple)
    if merge:
        sums, maxs, dens, expand = mix
        mix_args = [*sums, *maxs, *dens, expand]
        mix_specs = ([pl.BlockSpec((N_SLABS, ROW_TILE, LANES), lambda i: (0, i, 0))] * len(sums)
                     + [row(LANES)] * (len(maxs) + len(dens)) + [_resident(expand.shape)])
    else:
        mix_args = [mix]
        mix_specs = [row(mix.shape[1])]
    return pl.pallas_call(
        functools.partial(_post_kernel, merge=merge),
        grid=(s // ROW_TILE,),
        in_specs=[row(D_MODEL), *mix_specs, _resident(wo.shape), _resident(gm.shape),
                  _resident(w1.shape), _resident(w2.shape), _resident(gp.shape),
                  _resident(wg.shape),
                  pl.BlockSpec((None, ROW_TILE, PLE_DIM), lambda i: (layer, i, 0)),
                  _resident(wp.shape)],
        out_specs=row(D_MODEL),
        out_shape=jax.ShapeDtypeStruct((s, D_MODEL), F32),
        compiler_params=_params(1),
        name="post_mixer",
    )(x, *mix_args, wo, gm, w1, w2, gp, wg, p, wp)


HALO = 8


def _up_conv_kernel(x_ref, g_ref, wup_ref, cw_ref, cb_ref, sk_ref, wq_ref, wk_ref, wv_ref, wg_ref,
                    bg_ref, q_ref, k_ref, v_ref, skx_ref, zg_ref, gt_ref, halo_sc):
    @pl.when(pl.program_id(0) == 0)
    def _():
        halo_sc[...] = jnp.zeros_like(halo_sc)

    hn = _rms(x_ref[...], g_ref[...]).astype(BF16)
    tm = hn.shape[0]
    for c in range(B_INNER // D_MODEL):
        cs = slice(c * D_MODEL, (c + 1) * D_MODEL)
        z = jnp.dot(hn, wup_ref[:, B_INNER + c * D_MODEL:B_INNER + (c + 1) * D_MODEL],
                    preferred_element_type=F32)
        zg_ref[:, cs] = jax.nn.silu(z)
    hrow = lax.broadcasted_iota(jnp.int32, (HALO, MXU_TILE), 0)
    gates = jnp.broadcast_to(bg_ref[...], gt_ref.shape)
    for t in range(B_INNER // MXU_TILE):
        ts = slice(t * MXU_TILE, (t + 1) * MXU_TILE)
        xm = jnp.dot(hn, wup_ref[:, ts], preferred_element_type=F32)
        halo = halo_sc[:, ts]
        halo_sc[:, ts] = xm[tm - HALO:]
        y = cb_ref[:, ts] + cw_ref[B_CONV - 1:B_CONV, ts] * xm
        for j in range(1, B_CONV):
            r = pltpu.roll(xm, j, 0)
            hr = pltpu.roll(halo, j, 0)
            top = jnp.where(hrow < j, hr, r[:HALO])
            shifted = jnp.concatenate([top, r[HALO:]], axis=0)
            y = y + cw_ref[B_CONV - 1 - j:B_CONV - j, ts] * shifted
        xc = jax.nn.silu(y)
        skx_ref[:, ts] = sk_ref[:, ts] * xc
        xcb = xc.astype(BF16)
        qf = jnp.dot(xcb, wq_ref[t], preferred_element_type=F32)
        kf = jnp.dot(xcb, wk_ref[t], preferred_element_type=F32)
        vf = jnp.dot(xm.astype(BF16), wv_ref[t], preferred_element_type=F32)
        qb, kb, vb = qf.astype(BF16), kf.astype(BF16), vf.astype(BF16)
        q_ref[:, ts] = qb
        k_ref[:, ts] = (kf * B_HEAD_DIM ** -0.5).astype(BF16)
        v_ref[:, ts] = vb
        for part, val in enumerate((qb, kb, vb)):
            ws = slice(part * B_INNER + t * MXU_TILE, part * B_INNER + (t + 1) * MXU_TILE)
            gates = gates + jnp.dot(val, wg_ref[ws, :], preferred_element_type=F32)
    gt_ref[...] = gates


def _up_conv(x, g, wup, cw, cb, sk, wq, wk, wv, wg, bg):
    s = x.shape[0]
    row = lambda: pl.BlockSpec((ROW_TILE, B_INNER), lambda i: (i, 0))
    return pl.pallas_call(
        _up_conv_kernel,
        grid=(s // ROW_TILE,),
        in_specs=[pl.BlockSpec((ROW_TILE, D_MODEL), lambda i: (i, 0)),
                  _resident(g.shape), _resident(wup.shape),
                  _resident(cw.shape), _resident(cb.shape), _resident(sk.shape),
                  _resident(wq.shape), _resident(wk.shape), _resident(wv.shape),
                  _resident(wg.shape), _resident(bg.shape)],
        out_specs=[row(), row(), row(), row(), row(),
                   pl.BlockSpec((ROW_TILE, LANES), lambda i: (i, 0))],
        out_shape=[jax.ShapeDtypeStruct((s, B_INNER), BF16)] * 3
        + [jax.ShapeDtypeStruct((s, B_INNER), F32)] * 2 + [jax.ShapeDtypeStruct((s, LANES), F32)],
        scratch_shapes=[pltpu.VMEM((HALO, B_INNER), F32)],
        compiler_params=_params(1),
        name="up_conv_qkv_gates",
    )(x, g, wup, cw, cb, sk, wq, wk, wv, wg, bg)


N_COLS = B_HEAD_DIM + LANES


def _mlstm_kernel(q_ref, k_ref, v_ref, g_ref, skx_ref, zg_ref, tri_ref, hg_ref, o_ref,
                  c_sc, cb_sc, m_sc):
    @pl.when(pl.program_id(0) == 0)
    def _():
        c_sc[...] = jnp.zeros_like(c_sc)
        cb_sc[...] = jnp.zeros_like(cb_sc)
        m_sc[...] = jnp.full_like(m_sc, NEG_INF)

    chunk = q_ref.shape[0]
    g_all = g_ref[...]
    lf_all = jax.nn.log_sigmoid(g_all)
    b_all = jnp.dot(tri_ref[...], lf_all, preferred_element_type=F32,
                    precision=lax.Precision.HIGHEST)
    b_t = b_all.T
    g_t = g_all.T
    row = lax.broadcasted_iota(jnp.int32, (chunk, chunk), 0)
    col = lax.broadcasted_iota(jnp.int32, (chunk, chunk), 1)
    causal = col <= row
    ones = jnp.ones((chunk, LANES), BF16)
    for h in range(B_HEADS):
        hs = slice(h * B_HEAD_DIM, (h + 1) * B_HEAD_DIM)
        i_col = g_all[:, h:h + 1]
        b_col = b_all[:, B_HEADS + h:B_HEADS + h + 1]
        i_row = g_t[h:h + 1, :]
        b_row = b_t[B_HEADS + h:B_HEADS + h + 1, :]
        m_prev = m_sc[h, 0:1, 0:1]
        dmat = jnp.where(causal, b_col - b_row + i_row, NEG_INF)
        inter = b_col + m_prev
        m_t = jnp.maximum(inter, jnp.max(dmat, axis=-1, keepdims=True))
        qh = q_ref[:, hs]
        kh = k_ref[:, hs]
        vh = v_ref[:, hs]
        s = lax.dot_general(qh, kh, NT_DIMS, preferred_element_type=F32) * jnp.exp(dmat - m_t)
        sc = jnp.exp(inter - m_t)
        qc = jnp.dot(qh, cb_sc[h], preferred_element_type=F32)
        num = sc * qc[:, :B_HEAD_DIM] + jnp.dot(s.astype(BF16), vh, preferred_element_type=F32)
        den = sc * qc[:, B_HEAD_DIM:B_HEAD_DIM + 1] + jnp.sum(s, axis=-1, keepdims=True)
        hv = num / jnp.maximum(jnp.abs(den), jnp.exp(-m_t))
        b_last = b_col[chunk - 1:chunk, :]
        g_col = b_last - b_col + i_col
        m_new = jnp.maximum(b_last + m_prev, jnp.max(g_col, axis=0, keepdims=True))
        decay = jnp.exp(b_last + m_prev - m_new)
        wkb = (kh.astype(F32) * jnp.exp(g_col - m_new)).astype(BF16)
        vx = jnp.concatenate([vh, ones], axis=1)
        for rb in range(B_HEAD_DIM // LANES):
            rs = slice(rb * LANES, (rb + 1) * LANES)
            c_new = decay * c_sc[h, rs, :] + lax.dot_general(wkb[:, rs], vx, TN_DIMS,
                                                             preferred_element_type=F32)
            c_sc[h, rs, :] = c_new
            cb_sc[h, rs, :] = c_new.astype(BF16)
        m_sc[h] = jnp.broadcast_to(m_new, m_sc.shape[1:])
        o_ref[:, hs] = ((_rms(hv, hg_ref[:, hs]) + skx_ref[:, hs]) * zg_ref[:, hs]).astype(BF16)


def _mlstm_scan(q, k, v, gates, skx, zg, tri, hg):
    s = q.shape[0]
    blk = lambda: pl.BlockSpec((SCAN_CHUNK, B_INNER), lambda c: (c, 0))
    return pl.pallas_call(
        _mlstm_kernel,
        grid=(s // SCAN_CHUNK,),
        in_specs=[blk(), blk(), blk(), pl.BlockSpec((SCAN_CHUNK, LANES), lambda c: (c, 0)),
                  blk(), blk(), _resident(tri.shape), _resident(hg.shape)],
        out_specs=blk(),
        out_shape=jax.ShapeDtypeStruct((s, B_INNER), BF16),
        scratch_shapes=[pltpu.VMEM((B_HEADS, B_HEAD_DIM, N_COLS), F32),
                        pltpu.VMEM((B_HEADS, B_HEAD_DIM, N_COLS), BF16),
                        pltpu.VMEM((B_HEADS, 8, LANES), F32)],
        compiler_params=_params(1),
        name="mlstm_scan",
    )(q, k, v, gates, skx, zg, tri, hg)


def _block_diag_tiles(w):
    nblk, blk, _ = w.shape
    rows = w.reshape(nblk * blk // MXU_TILE, MXU_TILE, blk)
    idx = jnp.arange(MXU_TILE) // blk
    same_block = (idx[:, None] == idx[None, :]).astype(w.dtype)
    return (jnp.tile(rows, (1, 1, MXU_TILE // blk)) * same_block[None]).astype(BF16)


def _row(v):
    return v.reshape(1, -1).astype(F32)


def kernel(x, p, a_norm, a_w_qkv, a_q_gain, a_k_gain, a_w_o, b_norm, b_w_up, b_conv_w, b_conv_b,
           b_w_q, b_w_k, b_w_v, b_w_gate, b_b_gate, b_h_gain, b_skip, b_w_down, mlp_norm, mlp_w1,
           mlp_w2, ple_norm, ple_w_gate, ple_w_proj):
    bsz, seq, _ = x.shape
    assert bsz == 1
    xs = x.reshape(seq, D_MODEL)
    ps = p.reshape(p.shape[0], seq, PLE_DIM)

    qg = jnp.tile(a_q_gain[0], (1, A_HEADS)).reshape(A_N_GROUPS, 1, D_MODEL) * A_HEAD_DIM ** -0.5
    kg = jnp.tile(a_k_gain[0], (1, A_HEADS)).reshape(A_N_GROUPS, 1, D_MODEL)
    seg = np.arange(MXU_TILE) // A_HEAD_DIM
    ones = (seg[:, None] == seg[None, :]).astype(BF16)
    low = np.broadcast_to(np.arange(LANES) < A_HEAD_DIM, (Q_BLOCK, LANES))
    hmask = np.stack([low, ~low]).astype(BF16)
    expand = (np.arange(LANES)[:, None] == np.arange(D_MODEL)[None, :] // A_HEAD_DIM).astype(BF16)
    qkv_groups = _qkv_proj(xs, _row(a_norm[0]), a_w_qkv[0].astype(BF16), qg, kg, ones)
    stats = [_attention_group(qkv_groups[grp], grp, dil, _attn_bias(dil), hmask)
             for grp, (_, dil) in enumerate(A_GROUPS)]
    sums, maxs, dens = zip(*stats)
    xs = _post_mixer(xs, (sums, maxs, dens, expand), a_w_o[0].astype(BF16), _row(mlp_norm[0]),
                     mlp_w1[0].astype(BF16), mlp_w2[0].astype(BF16), _row(ple_norm[0]),
                     ple_w_gate[0].astype(BF16), ps, 0, ple_w_proj[0].astype(BF16))

    wg = jnp.zeros((3 * B_INNER, LANES), F32).at[:, :2 * B_HEADS].set(b_w_gate[0]).astype(BF16)
    bg = jnp.zeros((1, LANES), F32).at[0, :2 * B_HEADS].set(b_b_gate[0])
    q, k, v, skx, zg, gates = _up_conv(xs, _row(b_norm[0]), b_w_up[0].astype(BF16), b_conv_w[0],
                                       _row(b_conv_b[0]), _row(b_skip[0]),
                                       _block_diag_tiles(b_w_q[0]), _block_diag_tiles(b_w_k[0]),
                                       _block_diag_tiles(b_w_v[0]), wg, bg)
    tri = (np.arange(SCAN_CHUNK)[:, None] >= np.arange(SCAN_CHUNK)[None, :]).astype(np.float32)
    mix = _mlstm_scan(q, k, v, gates, skx, zg, tri, _row(b_h_gain[0]))
    xs = _post_mixer(xs, mix, b_w_down[0].astype(BF16), _row(mlp_norm[1]),
                     mlp_w1[1].astype(BF16), mlp_w2[1].astype(BF16), _row(ple_norm[1]),
                     ple_w_gate[1].astype(BF16), ps, 1, ple_w_proj[1].astype(BF16))
    return xs.reshape(bsz, seq, D_MODEL)
```

```python
import functools

import jax
import jax.numpy as jnp
import numpy as np
from jax import lax
from jax.experimental import pallas as pl
from jax.experimental.pallas import tpu as pltpu

F32 = jnp.float32
BF16 = jnp.bfloat16

D_MODEL = 1024
A_HEADS = 16
A_HEAD_DIM = D_MODEL // A_HEADS
A_GROUPS = ((128, 1), (512, 4), (2048, 16))
A_N_GROUPS = len(A_GROUPS)
Q_BLOCK = 128
TILES_PER_STEP = 8
B_INNER = 2 * D_MODEL
B_HEADS = 4
B_HEAD_DIM = B_INNER // B_HEADS
B_CONV = 4
B_QKV_BLOCK = 4
SCAN_CHUNK = 256
D_FF = 4 * D_MODEL
PLE_DIM = 256
EPS = 1e-6
NEG_INF = -1e30

MXU_TILE = 256
LANES = 128
N_SLABS = D_MODEL // LANES
VMEM_LIMIT_BYTES = 60000 * 1024
ROW_TILE = 512

NT_DIMS = (((1,), (1,)), ((), ()))
TN_DIMS = (((0,), (0,)), ((), ()))


def _params(n_axes):
    return pltpu.CompilerParams(dimension_semantics=("arbitrary",) * n_axes,
                                vmem_limit_bytes=VMEM_LIMIT_BYTES)


def _resident(shape):
    return pl.BlockSpec(shape, lambda *_: (0,) * len(shape), pipeline_mode=pl.Buffered(1))


def _rms(x, g):
    ms = jnp.mean(x * x, axis=-1, keepdims=True)
    return x * lax.rsqrt(ms + EPS) * g


def _qkv_kernel(x_ref, g_ref, w_ref, qg_ref, kg_ref, ones_ref, *rest):
    out_refs, hn_sc = rest[:A_N_GROUPS], rest[A_N_GROUPS]
    hn = _rms(x_ref[...], g_ref[...])
    tm = hn.shape[0]
    for j in range(N_SLABS):
        hn_sc[j] = hn[:, j * LANES:(j + 1) * LANES]
    for grp, (_, dil) in enumerate(A_GROUPS):
        o_ref = out_refs[grp]
        n = tm // dil
        if dil == 1:
            hg = hn.astype(BF16)
        else:
            hg = jnp.concatenate(
                [jnp.concatenate([hn_sc[j, pl.ds(r, n, stride=dil), :] for j in range(N_SLABS)],
                                 axis=1) for r in range(dil)], axis=0).astype(BF16)
        for kind in range(3):
            c = 3 * grp + kind
            y = jnp.dot(hg, w_ref[:, c * D_MODEL:(c + 1) * D_MODEL], preferred_element_type=F32)
            if kind < 2:
                y2 = (y * y).astype(BF16)
                parts = [jnp.dot(y2[:, t * MXU_TILE:(t + 1) * MXU_TILE], ones_ref[...],
                                 preferred_element_type=F32) for t in range(D_MODEL // MXU_TILE)]
                ss = jnp.concatenate(parts, axis=1)
                gain = qg_ref[grp] if kind == 0 else kg_ref[grp]
                y = y * lax.rsqrt(ss * (1.0 / A_HEAD_DIM) + EPS) * gain
            yb = y.astype(BF16)
            ks = slice(kind * D_MODEL, (kind + 1) * D_MODEL)
            for r in range(dil):
                o_ref[r, :, ks] = yb[r * n:(r + 1) * n]


def _qkv_proj(x, g, w, qg, kg, ones):
    s = x.shape[0]
    out_specs, out_shape = [], []
    for _, dil in A_GROUPS:
        out_specs.append(pl.BlockSpec((dil, ROW_TILE // dil, 3 * D_MODEL), lambda i: (0, i, 0)))
        out_shape.append(jax.ShapeDtypeStruct((dil, s // dil, 3 * D_MODEL), BF16))
    return pl.pallas_call(
        _qkv_kernel,
        grid=(s // ROW_TILE,),
        in_specs=[pl.BlockSpec((ROW_TILE, D_MODEL), lambda i: (i, 0)),
                  _resident(g.shape), _resident(w.shape), _resident(qg.shape),
                  _resident(kg.shape), _resident(ones.shape)],
        out_specs=out_specs,
        out_shape=out_shape,
        scratch_shapes=[pltpu.VMEM((N_SLABS, ROW_TILE, LANES), F32)],
        compiler_params=_params(1),
        name="qkv_proj",
    )(x, g, w, qg, kg, ones)


def _attn_kernel(q_ref, kp_ref, kc_ref, vp_ref, vc_ref, bias_ref, hmask_ref, o_ref, m_ref, l_ref, *,
                 dil):
    n_res = q_ref.shape[0]
    n_tiles = q_ref.shape[1] // Q_BLOCK
    lane = lax.broadcasted_iota(jnp.int32, (Q_BLOCK, LANES), 1)
    lo = lane < A_HEAD_DIM
    for sub in range(n_res):
        res = pl.program_id(1) * n_res + sub
        for tile in range(n_tiles):
            cur = slice(tile * Q_BLOCK, (tile + 1) * Q_BLOCK)
            both = slice((tile - 1) * Q_BLOCK, (tile + 1) * Q_BLOCK)
            if dil > 1:
                rows = pl.ds(tile * Q_BLOCK * dil + res, Q_BLOCK, stride=dil)
            else:
                rows = cur
            if tile == 0:
                sel = (pl.program_id(0) == 0).astype(jnp.int32)
            else:
                sel = 0
            m_tile = jnp.zeros((Q_BLOCK, LANES), F32)
            l_tile = jnp.ones((Q_BLOCK, LANES), F32)
            for pair in range(A_HEADS // 2):
                cs = slice(pair * LANES, (pair + 1) * LANES)
                q2 = q_ref[sub, cur, cs]
                if tile == 0:
                    kcat = jnp.concatenate([kp_ref[sub, :, cs], kc_ref[sub, cur, cs]], axis=0)
                    vcat = jnp.concatenate([vp_ref[sub, :, cs], vc_ref[sub, cur, cs]], axis=0)
                else:
                    kcat = kc_ref[sub, both, cs]
                    vcat = vc_ref[sub, both, cs]
                qq = jnp.concatenate([q2 * hmask_ref[0], q2 * hmask_ref[1]], axis=0)
                s = lax.dot_general(qq, kcat, NT_DIMS, preferred_element_type=F32)
                s = s + bias_ref[sel, pair]
                m = jnp.max(s, axis=-1, keepdims=True)
                p = jnp.exp(s - m)
                l = jnp.sum(p, axis=-1, keepdims=True)
                u = jnp.dot(p.astype(BF16), vcat, preferred_element_type=F32)
                for hh in range(2):
                    hr = slice(hh * Q_BLOCK, (hh + 1) * Q_BLOCK)
                    m_tile = jnp.where(lane == 2 * pair + hh, m[hr], m_tile)
                    l_tile = jnp.where(lane == 2 * pair + hh, l[hr], l_tile)
                o_ref[pair, rows, :] = jnp.where(lo, u[:Q_BLOCK], u[Q_BLOCK:])
            m_ref[rows, :] = m_tile
            l_ref[rows, :] = l_tile


def _attn_bias(dil):
    slopes = np.asarray([2.0 ** (-8.0 * (h + 1) / A_HEADS) for h in range(A_HEADS)], np.float32)
    row = np.arange(Q_BLOCK)[:, None]
    col = np.arange(Q_BLOCK)[None, :]

    def table(steps, valid):
        dist = (steps * dil).astype(np.float32)
        return np.where(valid[None], -(slopes[:, None, None] * dist[None]), np.float32(NEG_INF))

    prev = table(Q_BLOCK + row - col, col >= row)
    cur = table(row - col, col <= row)
    normal = np.concatenate([prev, cur], axis=-1)
    firstb = np.concatenate([np.full_like(prev, NEG_INF), cur], axis=-1)
    tables = np.stack([normal, firstb]).astype(np.float32)
    return tables.reshape(2, A_HEADS // 2, 2 * Q_BLOCK, 2 * Q_BLOCK)


def _attention_group(qkv_g, grp, dil, bias, hmask):
    _, sd, _ = qkv_g.shape
    s = sd * dil
    n_res = min(TILES_PER_STEP, dil)
    n_tiles = TILES_PER_STEP // n_res
    blk = n_tiles * Q_BLOCK

    def col_spec(off, prev):
        if prev:
            return pl.BlockSpec((n_res, Q_BLOCK, D_MODEL),
                                lambda b, r: (r, jnp.maximum(b * n_tiles - 1, 0), off))
        return pl.BlockSpec((n_res, blk, D_MODEL), lambda b, r: (r, b, off))

    return pl.pallas_call(
        functools.partial(_attn_kernel, dil=dil),
        grid=(sd // blk, dil // n_res),
        in_specs=[col_spec(0, False), col_spec(1, True), col_spec(1, False),
                  col_spec(2, True), col_spec(2, False),
                  _resident(bias.shape), _resident(hmask.shape)],
        out_specs=[pl.BlockSpec((N_SLABS, blk * dil, LANES), lambda b, r: (0, b, 0)),
                   pl.BlockSpec((blk * dil, LANES), lambda b, r: (b, 0)),
                   pl.BlockSpec((blk * dil, LANES), lambda b, r: (b, 0))],
        out_shape=[jax.ShapeDtypeStruct((N_SLABS, s, LANES), F32),
                   jax.ShapeDtypeStruct((s, LANES), F32), jax.ShapeDtypeStruct((s, LANES), F32)],
        compiler_params=_params(2),
        name=f"dilated_attn_g{grp}",
    )(qkv_g, qkv_g, qkv_g, qkv_g, qkv_g, bias, hmask)


def _merge_groups(u_refs, m_refs, l_refs, expand_ref):
    ms = [r[...] for r in m_refs]
    top = functools.reduce(jnp.maximum, ms)
    es = [jnp.exp(v - top) for v in ms]
    den = functools.reduce(jnp.add, [e * r[...] for e, r in zip(es, l_refs)])
    merged = None
    for u_ref, e in zip(u_refs, es):
        wide = jnp.dot((e / den).astype(BF16), expand_ref[...], preferred_element_type=F32)
        u = jnp.concatenate([u_ref[j] for j in range(N_SLABS)], axis=1)
        merged = wide * u if merged is None else merged + wide * u
    return merged.astype(BF16)


def _post_kernel(*refs, merge):
    x_ref = refs[0]
    if merge:
        g = A_N_GROUPS
        mix = _merge_groups(refs[1:1 + g], refs[1 + g:1 + 2 * g], refs[1 + 2 * g:1 + 3 * g],
                            refs[1 + 3 * g])
        refs = refs[2 + 3 * g:]
    else:
        mix = refs[1][...]
        refs = refs[2:]
    wo_ref, gm_ref, w1_ref, w2_ref, gp_ref, wg_ref, p_ref, wp_ref, o_ref = refs
    x = x_ref[...] + jnp.dot(mix, wo_ref[...], preferred_element_type=F32)
    hn = _rms(x, gm_ref[...]).astype(BF16)
    acc = jnp.zeros_like(x)
    for c in range(D_FF // D_MODEL):
        cs = slice(c * D_MODEL, (c + 1) * D_MODEL)
        a = jnp.dot(hn, w1_ref[:, cs], preferred_element_type=F32)
        a = jnp.square(jnp.maximum(a, 0.0)).astype(BF16)
        acc = acc + jnp.dot(a, w2_ref[cs, :], preferred_element_type=F32)
    x = x + acc
    gate = jax.nn.sigmoid(jnp.dot(_rms(x, gp_ref[...]).astype(BF16), wg_ref[...],
                                  preferred_element_type=F32))
    emb = jnp.dot(p_ref[...].astype(BF16), wp_ref[...], preferred_element_type=F32)
    o_ref[...] = x + gate * emb


def _post_mixer(x, mix, wo, gm, w1, w2, gp, wg, p, layer, wp):
    s = x.shape[0]
    row = lambda width: pl.BlockSpec((ROW_TILE, width), lambda i: (i, 0))
    merge = isinstance(mix, tuple)
    if merge:
        sums, maxs, dens, expand = mix
        mix_args = [*sums, *maxs, *dens, expand]
        mix_specs = ([pl.BlockSpec((N_SLABS, ROW_TILE, LANES), lambda i: (0, i, 0))] * len(sums)
                     + [row(LANES)] * (len(maxs) + len(dens)) + [_resident(expand.shape)])
    else:
        mix_args = [mix]
        mix_specs = [row(mix.shape[1])]
    return pl.pallas_call(
        functools.partial(_post_kernel, merge=merge),
        grid=(s // ROW_TILE,),
        in_specs=[row(D_MODEL), *mix_specs, _resident(wo.shape), _resident(gm.shape),
                  _resident(w1.shape), _resident(w2.shape), _resident(gp.shape),
                  _resident(wg.shape),
                  pl.BlockSpec((None, ROW_TILE, PLE_DIM), lambda i: (layer, i, 0)),
                  _resident(wp.shape)],
        out_specs=row(D_MODEL),
        out_shape=jax.ShapeDtypeStruct((s, D_MODEL), F32),
        compiler_params=_params(1),
        name="post_mixer",
    )(x, *mix_args, wo, gm, w1, w2, gp, wg, p, wp)


HALO = 8


def _up_conv_kernel(x_ref, g_ref, wup_ref, cw_ref, cb_ref, sk_ref, wq_ref, wk_ref, wv_ref, wg_ref,
                    bg_ref, q_ref, k_ref, v_ref, skx_ref, zg_ref, gt_ref, halo_sc, ext_sc):
    @pl.when(pl.program_id(0) == 0)
    def _():
        halo_sc[...] = jnp.zeros_like(halo_sc)

    hn = _rms(x_ref[...], g_ref[...]).astype(BF16)
    tm = hn.shape[0]
    for c in range(B_INNER // D_MODEL):
        cs = slice(c * D_MODEL, (c + 1) * D_MODEL)
        z = jnp.dot(hn, wup_ref[:, B_INNER + c * D_MODEL:B_INNER + (c + 1) * D_MODEL],
                    preferred_element_type=F32)
        zg_ref[:, cs] = jax.nn.silu(z)
    gates = jnp.broadcast_to(bg_ref[...], gt_ref.shape)
    for t in range(B_INNER // MXU_TILE):
        ts = slice(t * MXU_TILE, (t + 1) * MXU_TILE)
        xm = jnp.dot(hn, wup_ref[:, ts], preferred_element_type=F32)
        ext_sc[:HALO, :] = halo_sc[:, ts]
        ext_sc[HALO:, :] = xm
        halo_sc[:, ts] = xm[tm - HALO:]
        y = cb_ref[:, ts] + cw_ref[B_CONV - 1:B_CONV, ts] * xm
        for j in range(1, B_CONV):
            y = y + cw_ref[B_CONV - 1 - j:B_CONV - j, ts] * ext_sc[pl.ds(HALO - j, tm), :]
        xc = jax.nn.silu(y)
        skx_ref[:, ts] = sk_ref[:, ts] * xc
        xcb = xc.astype(BF16)
        qf = jnp.dot(xcb, wq_ref[t], preferred_element_type=F32)
        kf = jnp.dot(xcb, wk_ref[t], preferred_element_type=F32)
        vf = jnp.dot(xm.astype(BF16), wv_ref[t], preferred_element_type=F32)
        qb, kb, vb = qf.astype(BF16), kf.astype(BF16), vf.astype(BF16)
        q_ref[:, ts] = qb
        k_ref[:, ts] = (kf * B_HEAD_DIM ** -0.5).astype(BF16)
        v_ref[:, ts] = vb
        for part, val in enumerate((qb, kb, vb)):
            ws = slice(part * B_INNER + t * MXU_TILE, part * B_INNER + (t + 1) * MXU_TILE)
            gates = gates + jnp.dot(val, wg_ref[ws, :], preferred_element_type=F32)
    gt_ref[...] = gates


def _up_conv(x, g, wup, cw, cb, sk, wq, wk, wv, wg, bg):
    s = x.shape[0]
    row = lambda: pl.BlockSpec((ROW_TILE, B_INNER), lambda i: (i, 0))
    return pl.pallas_call(
        _up_conv_kernel,
        grid=(s // ROW_TILE,),
        in_specs=[pl.BlockSpec((ROW_TILE, D_MODEL), lambda i: (i, 0)),
                  _resident(g.shape), _resident(wup.shape),
                  _resident(cw.shape), _resident(cb.shape), _resident(sk.shape),
                  _resident(wq.shape), _resident(wk.shape), _resident(wv.shape),
                  _resident(wg.shape), _resident(bg.shape)],
        out_specs=[row(), row(), row(), row(), row(),
                   pl.BlockSpec((ROW_TILE, LANES), lambda i: (i, 0))],
        out_shape=[jax.ShapeDtypeStruct((s, B_INNER), BF16)] * 3
        + [jax.ShapeDtypeStruct((s, B_INNER), F32)] * 2 + [jax.ShapeDtypeStruct((s, LANES), F32)],
        scratch_shapes=[pltpu.VMEM((HALO, B_INNER), F32),
                        pltpu.VMEM((HALO + ROW_TILE, MXU_TILE), F32)],
        compiler_params=_params(1),
        name="up_conv_qkv_gates",
    )(x, g, wup, cw, cb, sk, wq, wk, wv, wg, bg)


N_COLS = B_HEAD_DIM + LANES


def _mlstm_kernel(q_ref, k_ref, v_ref, g_ref, skx_ref, zg_ref, tri_ref, hg_ref, o_ref,
                  c_sc, cb_sc, m_sc):
    @pl.when(pl.program_id(0) == 0)
    def _():
        c_sc[...] = jnp.zeros_like(c_sc)
        cb_sc[...] = jnp.zeros_like(cb_sc)
        m_sc[...] = jnp.full_like(m_sc, NEG_INF)

    chunk = q_ref.shape[0]
    g_all = g_ref[...]
    lf_all = jax.nn.log_sigmoid(g_all)
    b_all = jnp.dot(tri_ref[...], lf_all, preferred_element_type=F32,
                    precision=lax.Precision.HIGHEST)
    b_t = b_all.T
    g_t = g_all.T
    row = lax.broadcasted_iota(jnp.int32, (chunk, chunk), 0)
    col = lax.broadcasted_iota(jnp.int32, (chunk, chunk), 1)
    causal = col <= row
    ones = jnp.ones((chunk, LANES), BF16)
    for h in range(B_HEADS):
        hs = slice(h * B_HEAD_DIM, (h + 1) * B_HEAD_DIM)
        i_col = g_all[:, h:h + 1]
        b_col = b_all[:, B_HEADS + h:B_HEADS + h + 1]
        i_row = g_t[h:h + 1, :]
        b_row = b_t[B_HEADS + h:B_HEADS + h + 1, :]
        m_prev = m_sc[h, 0:1, 0:1]
        dmat = jnp.where(causal, b_col - b_row + i_row, NEG_INF)
        inter = b_col + m_prev
        m_t = jnp.maximum(inter, jnp.max(dmat, axis=-1, keepdims=True))
        qh = q_ref[:, hs]
        kh = k_ref[:, hs]
        vh = v_ref[:, hs]
        s = lax.dot_general(qh, kh, NT_DIMS, preferred_element_type=F32) * jnp.exp(dmat - m_t)
        sc = jnp.exp(inter - m_t)
        qc = jnp.dot(qh, cb_sc[h], preferred_element_type=F32)
        num = sc * qc[:, :B_HEAD_DIM] + jnp.dot(s.astype(BF16), vh, preferred_element_type=F32)
        den = sc * qc[:, B_HEAD_DIM:B_HEAD_DIM + 1] + jnp.sum(s, axis=-1, keepdims=True)
        hv = num / jnp.maximum(jnp.abs(den), jnp.exp(-m_t))
        b_last = b_col[chunk - 1:chunk, :]
        g_col = b_last - b_col + i_col
        m_new = jnp.maximum(b_last + m_prev, jnp.max(g_col, axis=0, keepdims=True))
        decay = jnp.exp(b_last + m_prev - m_new)
        wkb = (kh.astype(F32) * jnp.exp(g_col - m_new)).astype(BF16)
        vx = jnp.concatenate([vh, ones], axis=1)
        for rb in range(B_HEAD_DIM // LANES):
            rs = slice(rb * LANES, (rb + 1) * LANES)
            c_new = decay * c_sc[h, rs, :] + lax.dot_general(wkb[:, rs], vx, TN_DIMS,
                                                             preferred_element_type=F32)
            c_sc[h, rs, :] = c_new
            cb_sc[h, rs, :] = c_new.astype(BF16)
        m_sc[h] = jnp.broadcast_to(m_new, m_sc.shape[1:])
        o_ref[:, hs] = ((_rms(hv, hg_ref[:, hs]) + skx_ref[:, hs]) * zg_ref[:, hs]).astype(BF16)


def _mlstm_scan(q, k, v, gates, skx, zg, tri, hg):
    s = q.shape[0]
    blk = lambda: pl.BlockSpec((SCAN_CHUNK, B_INNER), lambda c: (c, 0))
    return pl.pallas_call(
        _mlstm_kernel,
        grid=(s // SCAN_CHUNK,),
        in_specs=[blk(), blk(), blk(), pl.BlockSpec((SCAN_CHUNK, LANES), lambda c: (c, 0)),
                  blk(), blk(), _resident(tri.shape), _resident(hg.shape)],
        out_specs=blk(),
        out_shape=jax.ShapeDtypeStruct((s, B_INNER), BF16),
        scratch_shapes=[pltpu.VMEM((B_HEADS, B_HEAD_DIM, N_COLS), F32),
                        pltpu.VMEM((B_HEADS, B_HEAD_DIM, N_COLS), BF16),
                        pltpu.VMEM((B_HEADS, 8, LANES), F32)],
        compiler_params=_params(1),
        name="mlstm_scan",
    )(q, k, v, gates, skx, zg, tri, hg)


def _block_diag_tiles(w):
    nblk, blk, _ = w.shape
    rows = w.reshape(nblk * blk // MXU_TILE, MXU_TILE, blk)
    idx = jnp.arange(MXU_TILE) // blk
    same_block = (idx[:, None] == idx[None, :]).astype(w.dtype)
    return (jnp.tile(rows, (1, 1, MXU_TILE // blk)) * same_block[None]).astype(BF16)


def _row(v):
    return v.reshape(1, -1).astype(F32)


def kernel(x, p, a_norm, a_w_qkv, a_q_gain, a_k_gain, a_w_o, b_norm, b_w_up, b_conv_w, b_conv_b,
           b_w_q, b_w_k, b_w_v, b_w_gate, b_b_gate, b_h_gain, b_skip, b_w_down, mlp_norm, mlp_w1,
           mlp_w2, ple_norm, ple_w_gate, ple_w_proj):
    bsz, seq, _ = x.shape
    assert bsz == 1
    xs = x.reshape(seq, D_MODEL)
    ps = p.reshape(p.shape[0], seq, PLE_DIM)

    qg = jnp.tile(a_q_gain[0], (1, A_HEADS)).reshape(A_N_GROUPS, 1, D_MODEL) * A_HEAD_DIM ** -0.5
    kg = jnp.tile(a_k_gain[0], (1, A_HEADS)).reshape(A_N_GROUPS, 1, D_MODEL)
    seg = np.arange(MXU_TILE) // A_HEAD_DIM
    ones = (seg[:, None] == seg[None, :]).astype(BF16)
    low = np.broadcast_to(np.arange(LANES) < A_HEAD_DIM, (Q_BLOCK, LANES))
    hmask = np.stack([low, ~low]).astype(BF16)
    expand = (np.arange(LANES)[:, None] == np.arange(D_MODEL)[None, :] // A_HEAD_DIM).astype(BF16)
    qkv_groups = _qkv_proj(xs, _row(a_norm[0]), a_w_qkv[0].astype(BF16), qg, kg, ones)
    stats = [_attention_group(qkv_groups[grp], grp, dil, _attn_bias(dil), hmask)
             for grp, (_, dil) in enumerate(A_GROUPS)]
    sums, maxs, dens = zip(*stats)
    xs = _post_mixer(xs, (sums, maxs, dens, expand), a_w_o[0].astype(BF16), _row(mlp_norm[0]),
                     mlp_w1[0].astype(BF16), mlp_w2[0].astype(BF16), _row(ple_norm[0]),
                     ple_w_gate[0].astype(BF16), ps, 0, ple_w_proj[0].astype(BF16))

    wg = jnp.zeros((3 * B_INNER, LANES), F32).at[:, :2 * B_HEADS].set(b_w_gate[0]).astype(BF16)
    bg = jnp.zeros((1, LANES), F32).at[0, :2 * B_HEADS].set(b_b_gate[0])
    q, k, v, skx, zg, gates = _up_conv(xs, _row(b_norm[0]), b_w_up[0].astype(BF16), b_conv_w[0],
                                       _row(b_conv_b[0]), _row(b_skip[0]),
                                       _block_diag_tiles(b_w_q[0]), _block_diag_tiles(b_w_k[0]),
                                       _block_diag_tiles(b_w_v[0]), wg, bg)
    tri = (np.arange(SCAN_CHUNK)[:, None] >= np.arange(SCAN_CHUNK)[None, :]).astype(np.float32)
    mix = _mlstm_scan(q, k, v, gates, skx, zg, tri, _row(b_h_gain[0]))
    xs = _post_mixer(xs, mix, b_w_down[0].astype(BF16), _row(mlp_norm[1]),
                     mlp_w1[1].astype(BF16), mlp_w2[1].astype(BF16), _row(ple_norm[1]),
                     ple_w_gate[1].astype(BF16), ps, 1, ple_w_proj[1].astype(BF16))
    return xs.reshape(bsz, seq, D_MODEL)
```

```python
import functools

import jax
import jax.numpy as jnp
import numpy as np
from jax import lax
from jax.experimental import pallas as pl
from jax.experimental.pallas import tpu as pltpu

F32 = jnp.float32
BF16 = jnp.bfloat16

D_MODEL = 1024
A_HEADS = 16
A_HEAD_DIM = D_MODEL // A_HEADS
A_GROUPS = ((128, 1), (512, 4), (2048, 16))
A_N_GROUPS = len(A_GROUPS)
Q_BLOCK = 128
TILES_PER_STEP = 8
B_INNER = 2 * D_MODEL
B_HEADS = 4
B_HEAD_DIM = B_INNER // B_HEADS
B_CONV = 4
B_QKV_BLOCK = 4
SCAN_CHUNK = 256
D_FF = 4 * D_MODEL
PLE_DIM = 256
EPS = 1e-6
NEG_INF = -1e30

MXU_TILE = 256
LANES = 128
N_SLABS = D_MODEL // LANES
VMEM_LIMIT_BYTES = 60000 * 1024
ROW_TILE = 512

NT_DIMS = (((1,), (1,)), ((), ()))
TN_DIMS = (((0,), (0,)), ((), ()))


def _params(n_axes):
    return pltpu.CompilerParams(dimension_semantics=("arbitrary",) * n_axes,
                                vmem_limit_bytes=VMEM_LIMIT_BYTES)


def _resident(shape):
    return pl.BlockSpec(shape, lambda *_: (0,) * len(shape), pipeline_mode=pl.Buffered(1))


def _rms(x, g):
    ms = jnp.mean(x * x, axis=-1, keepdims=True)
    return x * lax.rsqrt(ms + EPS) * g


def _rider_specs(jobs, n_steps, step_of):
    in_specs, out_specs, out_shape = [], [], []
    for w, layer in jobs:
        _, r, c = w.shape
        rows = r // n_steps
        assert rows * n_steps == r and rows % 16 == 0, (w.shape, n_steps)
        in_specs.append(pl.BlockSpec((None, rows, c),
                                     lambda *g, layer=layer: (layer, step_of(*g), 0)))
        out_specs.append(pl.BlockSpec((rows, c), lambda *g: (step_of(*g), 0)))
        out_shape.append(jax.ShapeDtypeStruct((r, c), BF16))
    return in_specs, out_specs, out_shape


def _run_riders(src_refs, dst_refs):
    for src, dst in zip(src_refs, dst_refs):
        dst[...] = src[...].astype(BF16)


def _qkv_kernel(x_ref, g_ref, w_ref, qg_ref, kg_ref, ones_ref, *rest):
    out_refs, hn_sc = rest[:A_N_GROUPS], rest[A_N_GROUPS]
    hn = _rms(x_ref[...], g_ref[...])
    tm = hn.shape[0]
    for j in range(N_SLABS):
        hn_sc[j] = hn[:, j * LANES:(j + 1) * LANES]
    for grp, (_, dil) in enumerate(A_GROUPS):
        o_ref = out_refs[grp]
        n = tm // dil
        if dil == 1:
            hg = hn.astype(BF16)
        else:
            hg = jnp.concatenate(
                [jnp.concatenate([hn_sc[j, pl.ds(r, n, stride=dil), :] for j in range(N_SLABS)],
                                 axis=1) for r in range(dil)], axis=0).astype(BF16)
        for kind in range(3):
            c = 3 * grp + kind
            y = jnp.dot(hg, w_ref[:, c * D_MODEL:(c + 1) * D_MODEL], preferred_element_type=F32)
            if kind < 2:
                y2 = (y * y).astype(BF16)
                parts = [jnp.dot(y2[:, t * MXU_TILE:(t + 1) * MXU_TILE], ones_ref[...],
                                 preferred_element_type=F32) for t in range(D_MODEL // MXU_TILE)]
                ss = jnp.concatenate(parts, axis=1)
                gain = qg_ref[grp] if kind == 0 else kg_ref[grp]
                y = y * lax.rsqrt(ss * (1.0 / A_HEAD_DIM) + EPS) * gain
            yb = y.astype(BF16)
            ks = slice(kind * D_MODEL, (kind + 1) * D_MODEL)
            for r in range(dil):
                o_ref[r, :, ks] = yb[r * n:(r + 1) * n]


def _qkv_proj(x, g, w, qg, kg, ones):
    s = x.shape[0]
    out_specs, out_shape = [], []
    for _, dil in A_GROUPS:
        out_specs.append(pl.BlockSpec((dil, ROW_TILE // dil, 3 * D_MODEL), lambda i: (0, i, 0)))
        out_shape.append(jax.ShapeDtypeStruct((dil, s // dil, 3 * D_MODEL), BF16))
    return pl.pallas_call(
        _qkv_kernel,
        grid=(s // ROW_TILE,),
        in_specs=[pl.BlockSpec((ROW_TILE, D_MODEL), lambda i: (i, 0)),
                  _resident(g.shape), _resident(w.shape), _resident(qg.shape),
                  _resident(kg.shape), _resident(ones.shape)],
        out_specs=out_specs,
        out_shape=out_shape,
        scratch_shapes=[pltpu.VMEM((N_SLABS, ROW_TILE, LANES), F32)],
        compiler_params=_params(1),
        name="qkv_proj",
    )(x, g, w, qg, kg, ones)


def _attn_kernel(q_ref, kp_ref, kc_ref, vp_ref, vc_ref, bias_ref, hmask_ref, *rest, dil, n_riders):
    o_ref, m_ref, l_ref = rest[n_riders:n_riders + 3]
    _run_riders(rest[:n_riders], rest[n_riders + 3:])
    n_res = q_ref.shape[0]
    n_tiles = q_ref.shape[1] // Q_BLOCK
    lane = lax.broadcasted_iota(jnp.int32, (Q_BLOCK, LANES), 1)
    lo = lane < A_HEAD_DIM
    for sub in range(n_res):
        res = pl.program_id(1) * n_res + sub
        for tile in range(n_tiles):
            cur = slice(tile * Q_BLOCK, (tile + 1) * Q_BLOCK)
            both = slice((tile - 1) * Q_BLOCK, (tile + 1) * Q_BLOCK)
            if dil > 1:
                rows = pl.ds(tile * Q_BLOCK * dil + res, Q_BLOCK, stride=dil)
            else:
                rows = cur
            if tile == 0:
                sel = (pl.program_id(0) == 0).astype(jnp.int32)
            else:
                sel = 0
            m_tile = jnp.zeros((Q_BLOCK, LANES), F32)
            l_tile = jnp.ones((Q_BLOCK, LANES), F32)
            for pair in range(A_HEADS // 2):
                cs = slice(pair * LANES, (pair + 1) * LANES)
                q2 = q_ref[sub, cur, cs]
                if tile == 0:
                    kcat = jnp.concatenate([kp_ref[sub, :, cs], kc_ref[sub, cur, cs]], axis=0)
                    vcat = jnp.concatenate([vp_ref[sub, :, cs], vc_ref[sub, cur, cs]], axis=0)
                else:
                    kcat = kc_ref[sub, both, cs]
                    vcat = vc_ref[sub, both, cs]
                qq = jnp.concatenate([q2 * hmask_ref[0], q2 * hmask_ref[1]], axis=0)
                s = lax.dot_general(qq, kcat, NT_DIMS, preferred_element_type=F32)
                s = s + bias_ref[sel, pair]
                m = jnp.max(s, axis=-1, keepdims=True)
                p = jnp.exp(s - m)
                l = jnp.sum(p, axis=-1, keepdims=True)
                u = jnp.dot(p.astype(BF16), vcat, preferred_element_type=F32)
                for hh in range(2):
                    hr = slice(hh * Q_BLOCK, (hh + 1) * Q_BLOCK)
                    m_tile = jnp.where(lane == 2 * pair + hh, m[hr], m_tile)
                    l_tile = jnp.where(lane == 2 * pair + hh, l[hr], l_tile)
                o_ref[pair, rows, :] = jnp.where(lo, u[:Q_BLOCK], u[Q_BLOCK:])
            m_ref[rows, :] = m_tile
            l_ref[rows, :] = l_tile


def _attn_bias(dil):
    slopes = np.asarray([2.0 ** (-8.0 * (h + 1) / A_HEADS) for h in range(A_HEADS)], np.float32)
    row = np.arange(Q_BLOCK)[:, None]
    col = np.arange(Q_BLOCK)[None, :]

    def table(steps, valid):
        dist = (steps * dil).astype(np.float32)
        return np.where(valid[None], -(slopes[:, None, None] * dist[None]), np.float32(NEG_INF))

    prev = table(Q_BLOCK + row - col, col >= row)
    cur = table(row - col, col <= row)
    normal = np.concatenate([prev, cur], axis=-1)
    firstb = np.concatenate([np.full_like(prev, NEG_INF), cur], axis=-1)
    tables = np.stack([normal, firstb]).astype(np.float32)
    return tables.reshape(2, A_HEADS // 2, 2 * Q_BLOCK, 2 * Q_BLOCK)


def _attention_group(qkv_g, grp, dil, bias, hmask, riders):
    _, sd, _ = qkv_g.shape
    s = sd * dil
    n_res = min(TILES_PER_STEP, dil)
    n_tiles = TILES_PER_STEP // n_res
    blk = n_tiles * Q_BLOCK

    def col_spec(off, prev):
        if prev:
            return pl.BlockSpec((n_res, Q_BLOCK, D_MODEL),
                                lambda b, r: (r, jnp.maximum(b * n_tiles - 1, 0), off))
        return pl.BlockSpec((n_res, blk, D_MODEL), lambda b, r: (r, b, off))

    grid = (sd // blk, dil // n_res)
    r_in, r_out, r_shape = _rider_specs(riders, grid[0] * grid[1], lambda b, r: b * grid[1] + r)
    outs = pl.pallas_call(
        functools.partial(_attn_kernel, dil=dil, n_riders=len(riders)),
        grid=grid,
        in_specs=[col_spec(0, False), col_spec(1, True), col_spec(1, False),
                  col_spec(2, True), col_spec(2, False),
                  _resident(bias.shape), _resident(hmask.shape), *r_in],
        out_specs=[pl.BlockSpec((N_SLABS, blk * dil, LANES), lambda b, r: (0, b, 0)),
                   pl.BlockSpec((blk * dil, LANES), lambda b, r: (b, 0)),
                   pl.BlockSpec((blk * dil, LANES), lambda b, r: (b, 0)), *r_out],
        out_shape=[jax.ShapeDtypeStruct((N_SLABS, s, LANES), F32),
                   jax.ShapeDtypeStruct((s, LANES), F32), jax.ShapeDtypeStruct((s, LANES), F32),
                   *r_shape],
        compiler_params=_params(2),
        name=f"dilated_attn_g{grp}",
    )(qkv_g, qkv_g, qkv_g, qkv_g, qkv_g, bias, hmask, *[w for w, _ in riders])
    return outs[:3], outs[3:]


def _merge_groups(u_refs, m_refs, l_refs, expand_ref):
    ms = [r[...] for r in m_refs]
    top = functools.reduce(jnp.maximum, ms)
    es = [jnp.exp(v - top) for v in ms]
    den = functools.reduce(jnp.add, [e * r[...] for e, r in zip(es, l_refs)])
    merged = None
    for u_ref, e in zip(u_refs, es):
        wide = jnp.dot((e / den).astype(BF16), expand_ref[...], preferred_element_type=F32)
        u = jnp.concatenate([u_ref[j] for j in range(N_SLABS)], axis=1)
        merged = wide * u if merged is None else merged + wide * u
    return merged.astype(BF16)


def _post_kernel(*refs, merge):
    x_ref = refs[0]
    if merge:
        g = A_N_GROUPS
        mix = _merge_groups(refs[1:1 + g], refs[1 + g:1 + 2 * g], refs[1 + 2 * g:1 + 3 * g],
                            refs[1 + 3 * g])
        refs = refs[2 + 3 * g:]
    else:
        mix = refs[1][...]
        refs = refs[2:]
    wo_ref, gm_ref, w1_ref, w2_ref, gp_ref, wg_ref, p_ref, wp_ref, o_ref = refs
    x = x_ref[...] + jnp.dot(mix, wo_ref[...], preferred_element_type=F32)
    hn = _rms(x, gm_ref[...]).astype(BF16)
    acc = jnp.zeros_like(x)
    for c in range(D_FF // D_MODEL):
        cs = slice(c * D_MODEL, (c + 1) * D_MODEL)
        a = jnp.dot(hn, w1_ref[:, cs], preferred_element_type=F32)
        a = jnp.square(jnp.maximum(a, 0.0)).astype(BF16)
        acc = acc + jnp.dot(a, w2_ref[cs, :], preferred_element_type=F32)
    x = x + acc
    gate = jax.nn.sigmoid(jnp.dot(_rms(x, gp_ref[...]).astype(BF16), wg_ref[...],
                                  preferred_element_type=F32))
    emb = jnp.dot(p_ref[...].astype(BF16), wp_ref[...], preferred_element_type=F32)
    o_ref[...] = x + gate * emb


def _post_mixer(x, mix, wo, gm, w1, w2, gp, wg, p, layer, wp):
    s = x.shape[0]
    row = lambda width: pl.BlockSpec((ROW_TILE, width), lambda i: (i, 0))
    merge = isinstance(mix, tuple)
    if merge:
        sums, maxs, dens, expand = mix
        mix_args = [*sums, *maxs, *dens, expand]
        mix_specs = ([pl.BlockSpec((N_SLABS, ROW_TILE, LANES), lambda i: (0, i, 0))] * len(sums)
                     + [row(LANES)] * (len(maxs) + len(dens)) + [_resident(expand.shape)])
    else:
        mix_args = [mix]
        mix_specs = [row(mix.shape[1])]
    return pl.pallas_call(
        functools.partial(_post_kernel, merge=merge),
        grid=(s // ROW_TILE,),
        in_specs=[row(D_MODEL), *mix_specs, _resident(wo.shape), _resident(gm.shape),
                  _resident(w1.shape), _resident(w2.shape), _resident(gp.shape),
                  _resident(wg.shape),
                  pl.BlockSpec((None, ROW_TILE, PLE_DIM), lambda i: (layer, i, 0)),
                  _resident(wp.shape)],
        out_specs=row(D_MODEL),
        out_shape=jax.ShapeDtypeStruct((s, D_MODEL), F32),
        compiler_params=_params(1),
        name="post_mixer",
    )(x, *mix_args, wo, gm, w1, w2, gp, wg, p, wp)


HALO = 8


def _up_conv_kernel(x_ref, g_ref, wup_ref, cw_ref, cb_ref, sk_ref, wq_ref, wk_ref, wv_ref, wg_ref,
                    bg_ref, q_ref, k_ref, v_ref, skx_ref, zg_ref, gt_ref, halo_sc, ext_sc):
    @pl.when(pl.program_id(0) == 0)
    def _():
        halo_sc[...] = jnp.zeros_like(halo_sc)

    hn = _rms(x_ref[...], g_ref[...]).astype(BF16)
    tm = hn.shape[0]
    for c in range(B_INNER // D_MODEL):
        cs = slice(c * D_MODEL, (c + 1) * D_MODEL)
        z = jnp.dot(hn, wup_ref[:, B_INNER + c * D_MODEL:B_INNER + (c + 1) * D_MODEL],
                    preferred_element_type=F32)
        zg_ref[:, cs] = jax.nn.silu(z)
    gates = jnp.broadcast_to(bg_ref[...], gt_ref.shape)
    for t in range(B_INNER // MXU_TILE):
        ts = slice(t * MXU_TILE, (t + 1) * MXU_TILE)
        xm = jnp.dot(hn, wup_ref[:, ts], preferred_element_type=F32)
        ext_sc[:HALO, :] = halo_sc[:, ts]
        ext_sc[HALO:, :] = xm
        halo_sc[:, ts] = xm[tm - HALO:]
        y = cb_ref[:, ts] + cw_ref[B_CONV - 1:B_CONV, ts] * xm
        for j in range(1, B_CONV):
            y = y + cw_ref[B_CONV - 1 - j:B_CONV - j, ts] * ext_sc[pl.ds(HALO - j, tm), :]
        xc = jax.nn.silu(y)
        skx_ref[:, ts] = sk_ref[:, ts] * xc
        xcb = xc.astype(BF16)
        qf = jnp.dot(xcb, wq_ref[t], preferred_element_type=F32)
        kf = jnp.dot(xcb, wk_ref[t], preferred_element_type=F32)
        vf = jnp.dot(xm.astype(BF16), wv_ref[t], preferred_element_type=F32)
        qb, kb, vb = qf.astype(BF16), kf.astype(BF16), vf.astype(BF16)
        q_ref[:, ts] = qb
        k_ref[:, ts] = (kf * B_HEAD_DIM ** -0.5).astype(BF16)
        v_ref[:, ts] = vb
        for part, val in enumerate((qb, kb, vb)):
            ws = slice(part * B_INNER + t * MXU_TILE, part * B_INNER + (t + 1) * MXU_TILE)
            gates = gates + jnp.dot(val, wg_ref[ws, :], preferred_element_type=F32)
    gt_ref[...] = gates


def _up_conv(x, g, wup, cw, cb, sk, wq, wk, wv, wg, bg):
    s = x.shape[0]
    row = lambda: pl.BlockSpec((ROW_TILE, B_INNER), lambda i: (i, 0))
    return pl.pallas_call(
        _up_conv_kernel,
        grid=(s // ROW_TILE,),
        in_specs=[pl.BlockSpec((ROW_TILE, D_MODEL), lambda i: (i, 0)),
                  _resident(g.shape), _resident(wup.shape),
                  _resident(cw.shape), _resident(cb.shape), _resident(sk.shape),
                  _resident(wq.shape), _resident(wk.shape), _resident(wv.shape),
                  _resident(wg.shape), _resident(bg.shape)],
        out_specs=[row(), row(), row(), row(), row(),
                   pl.BlockSpec((ROW_TILE, LANES), lambda i: (i, 0))],
        out_shape=[jax.ShapeDtypeStruct((s, B_INNER), BF16)] * 3
        + [jax.ShapeDtypeStruct((s, B_INNER), F32)] * 2 + [jax.ShapeDtypeStruct((s, LANES), F32)],
        scratch_shapes=[pltpu.VMEM((HALO, B_INNER), F32),
                        pltpu.VMEM((HALO + ROW_TILE, MXU_TILE), F32)],
        compiler_params=_params(1),
        name="up_conv_qkv_gates",
    )(x, g, wup, cw, cb, sk, wq, wk, wv, wg, bg)


N_COLS = B_HEAD_DIM + LANES


def _mlstm_kernel(q_ref, k_ref, v_ref, g_ref, skx_ref, zg_ref, tri_ref, hg_ref, *rest, n_riders):
    o_ref = rest[n_riders]
    c_sc, cb_sc, m_sc = rest[2 * n_riders + 1:]
    _run_riders(rest[:n_riders], rest[n_riders + 1:2 * n_riders + 1])

    @pl.when(pl.program_id(0) == 0)
    def _():
        c_sc[...] = jnp.zeros_like(c_sc)
        cb_sc[...] = jnp.zeros_like(cb_sc)
        m_sc[...] = jnp.full_like(m_sc, NEG_INF)

    chunk = q_ref.shape[0]
    g_all = g_ref[...]
    lf_all = jax.nn.log_sigmoid(g_all)
    b_all = jnp.dot(tri_ref[...], lf_all, preferred_element_type=F32,
                    precision=lax.Precision.HIGHEST)
    b_t = b_all.T
    g_t = g_all.T
    row = lax.broadcasted_iota(jnp.int32, (chunk, chunk), 0)
    col = lax.broadcasted_iota(jnp.int32, (chunk, chunk), 1)
    causal = col <= row
    ones = jnp.ones((chunk, LANES), BF16)
    for h in range(B_HEADS):
        hs = slice(h * B_HEAD_DIM, (h + 1) * B_HEAD_DIM)
        i_col = g_all[:, h:h + 1]
        b_col = b_all[:, B_HEADS + h:B_HEADS + h + 1]
        i_row = g_t[h:h + 1, :]
        b_row = b_t[B_HEADS + h:B_HEADS + h + 1, :]
        m_prev = m_sc[h, 0:1, 0:1]
        dmat = jnp.where(causal, b_col - b_row + i_row, NEG_INF)
        inter = b_col + m_prev
        m_t = jnp.maximum(inter, jnp.max(dmat, axis=-1, keepdims=True))
        qh = q_ref[:, hs]
        kh = k_ref[:, hs]
        vh = v_ref[:, hs]
        s = lax.dot_general(qh, kh, NT_DIMS, preferred_element_type=F32) * jnp.exp(dmat - m_t)
        sc = jnp.exp(inter - m_t)
        qc = jnp.dot(qh, cb_sc[h], preferred_element_type=F32)
        num = sc * qc[:, :B_HEAD_DIM] + jnp.dot(s.astype(BF16), vh, preferred_element_type=F32)
        den = sc * qc[:, B_HEAD_DIM:B_HEAD_DIM + 1] + jnp.sum(s, axis=-1, keepdims=True)
        hv = num / jnp.maximum(jnp.abs(den), jnp.exp(-m_t))
        b_last = b_col[chunk - 1:chunk, :]
        g_col = b_last - b_col + i_col
        m_new = jnp.maximum(b_last + m_prev, jnp.max(g_col, axis=0, keepdims=True))
        decay = jnp.exp(b_last + m_prev - m_new)
        wkb = (kh.astype(F32) * jnp.exp(g_col - m_new)).astype(BF16)
        vx = jnp.concatenate([vh, ones], axis=1)
        for rb in range(B_HEAD_DIM // LANES):
            rs = slice(rb * LANES, (rb + 1) * LANES)
            c_new = decay * c_sc[h, rs, :] + lax.dot_general(wkb[:, rs], vx, TN_DIMS,
                                                             preferred_element_type=F32)
            c_sc[h, rs, :] = c_new
            cb_sc[h, rs, :] = c_new.astype(BF16)
        m_sc[h] = jnp.broadcast_to(m_new, m_sc.shape[1:])
        o_ref[:, hs] = ((_rms(hv, hg_ref[:, hs]) + skx_ref[:, hs]) * zg_ref[:, hs]).astype(BF16)


def _mlstm_scan(q, k, v, gates, skx, zg, tri, hg, riders):
    s = q.shape[0]
    blk = lambda: pl.BlockSpec((SCAN_CHUNK, B_INNER), lambda c: (c, 0))
    r_in, r_out, r_shape = _rider_specs(riders, s // SCAN_CHUNK, lambda c: c)
    outs = pl.pallas_call(
        functools.partial(_mlstm_kernel, n_riders=len(riders)),
        grid=(s // SCAN_CHUNK,),
        in_specs=[blk(), blk(), blk(), pl.BlockSpec((SCAN_CHUNK, LANES), lambda c: (c, 0)),
                  blk(), blk(), _resident(tri.shape), _resident(hg.shape), *r_in],
        out_specs=[blk(), *r_out],
        out_shape=[jax.ShapeDtypeStruct((s, B_INNER), BF16), *r_shape],
        scratch_shapes=[pltpu.VMEM((B_HEADS, B_HEAD_DIM, N_COLS), F32),
                        pltpu.VMEM((B_HEADS, B_HEAD_DIM, N_COLS), BF16),
                        pltpu.VMEM((B_HEADS, 8, LANES), F32)],
        compiler_params=_params(1),
        name="mlstm_scan",
    )(q, k, v, gates, skx, zg, tri, hg, *[w for w, _ in riders])
    return outs[0], outs[1:]


def _block_diag_tiles(w):
    nblk, blk, _ = w.shape
    rows = w.reshape(nblk * blk // MXU_TILE, MXU_TILE, blk)
    idx = jnp.arange(MXU_TILE) // blk
    same_block = (idx[:, None] == idx[None, :]).astype(w.dtype)
    return (jnp.tile(rows, (1, 1, MXU_TILE // blk)) * same_block[None]).astype(BF16)


def _row(v):
    return v.reshape(1, -1).astype(F32)


def kernel(x, p, a_norm, a_w_qkv, a_q_gain, a_k_gain, a_w_o, b_norm, b_w_up, b_conv_w, b_conv_b,
           b_w_q, b_w_k, b_w_v, b_w_gate, b_b_gate, b_h_gain, b_skip, b_w_down, mlp_norm, mlp_w1,
           mlp_w2, ple_norm, ple_w_gate, ple_w_proj):
    bsz, seq, _ = x.shape
    assert bsz == 1
    xs = x.reshape(seq, D_MODEL)
    ps = p.reshape(p.shape[0], seq, PLE_DIM)

    qg = jnp.tile(a_q_gain[0], (1, A_HEADS)).reshape(A_N_GROUPS, 1, D_MODEL) * A_HEAD_DIM ** -0.5
    kg = jnp.tile(a_k_gain[0], (1, A_HEADS)).reshape(A_N_GROUPS, 1, D_MODEL)
    seg = np.arange(MXU_TILE) // A_HEAD_DIM
    ones = (seg[:, None] == seg[None, :]).astype(BF16)
    low = np.broadcast_to(np.arange(LANES) < A_HEAD_DIM, (Q_BLOCK, LANES))
    hmask = np.stack([low, ~low]).astype(BF16)
    expand = (np.arange(LANES)[:, None] == np.arange(D_MODEL)[None, :] // A_HEAD_DIM).astype(BF16)
    qkv_groups = _qkv_proj(xs, _row(a_norm[0]), a_w_qkv[0].astype(BF16), qg, kg, ones)
    riders = ([(mlp_w1, 0), (a_w_o, 0)],
              [(mlp_w2, 0), (ple_w_gate, 0), (ple_w_proj, 0)],
              [(b_w_up, 0), (ple_w_proj, 1)])
    stats, cast = zip(*[_attention_group(qkv_groups[grp], grp, dil, _attn_bias(dil), hmask,
                                         riders[grp]) for grp, (_, dil) in enumerate(A_GROUPS)])
    sums, maxs, dens = zip(*stats)
    (w1_0, wo_0), (w2_0, wg_0, wp_0), (wup, wp_1) = cast
    xs = _post_mixer(xs, (sums, maxs, dens, expand), wo_0, _row(mlp_norm[0]), w1_0, w2_0,
                     _row(ple_norm[0]), wg_0, ps, 0, wp_0)

    wg = jnp.zeros((3 * B_INNER, LANES), F32).at[:, :2 * B_HEADS].set(b_w_gate[0]).astype(BF16)
    bg = jnp.zeros((1, LANES), F32).at[0, :2 * B_HEADS].set(b_b_gate[0])
    q, k, v, skx, zg, gates = _up_conv(xs, _row(b_norm[0]), wup, b_conv_w[0],
                                       _row(b_conv_b[0]), _row(b_skip[0]),
                                       _block_diag_tiles(b_w_q[0]), _block_diag_tiles(b_w_k[0]),
                                       _block_diag_tiles(b_w_v[0]), wg, bg)
    tri = (np.arange(SCAN_CHUNK)[:, None] >= np.arange(SCAN_CHUNK)[None, :]).astype(np.float32)
    mix, (wo_1, w1_1, w2_1, wg_1) = _mlstm_scan(
        q, k, v, gates, skx, zg, tri, _row(b_h_gain[0]),
        [(b_w_down, 0), (mlp_w1, 1), (mlp_w2, 1), (ple_w_gate, 1)])
    xs = _post_mixer(xs, mix, wo_1, _row(mlp_norm[1]), w1_1, w2_1, _row(ple_norm[1]), wg_1, ps, 1,
                     wp_1)
    return xs.reshape(bsz, seq, D_MODEL)
```

```python
import functools

import jax
import jax.numpy as jnp
import numpy as np
from jax import lax
from jax.experimental import pallas as pl
from jax.experimental.pallas import tpu as pltpu

F32 = jnp.float32
BF16 = jnp.bfloat16

D_MODEL = 1024
A_HEADS = 16
A_HEAD_DIM = D_MODEL // A_HEADS
A_GROUPS = ((128, 1), (512, 4), (2048, 16))
A_N_GROUPS = len(A_GROUPS)
Q_BLOCK = 128
TILES_PER_STEP = 8
B_INNER = 2 * D_MODEL
B_HEADS = 4
B_HEAD_DIM = B_INNER // B_HEADS
B_CONV = 4
B_QKV_BLOCK = 4
SCAN_CHUNK = 256
D_FF = 4 * D_MODEL
PLE_DIM = 256
EPS = 1e-6
NEG_INF = -1e30

MXU_TILE = 256
LANES = 128
N_SLABS = D_MODEL // LANES
VMEM_LIMIT_BYTES = 60000 * 1024
ROW_TILE = 512

NT_DIMS = (((1,), (1,)), ((), ()))
TN_DIMS = (((0,), (0,)), ((), ()))


def _params(n_axes):
    return pltpu.CompilerParams(dimension_semantics=("arbitrary",) * n_axes,
                                vmem_limit_bytes=VMEM_LIMIT_BYTES)


def _resident(shape):
    return pl.BlockSpec(shape, lambda *_: (0,) * len(shape), pipeline_mode=pl.Buffered(1))


def _rms(x, g):
    ms = jnp.mean(x * x, axis=-1, keepdims=True)
    return x * lax.rsqrt(ms + EPS) * g


def _rider_specs(jobs, n_steps, step_of):
    in_specs, out_specs, out_shape = [], [], []
    for w, layer in jobs:
        _, r, c = w.shape
        rows = r // n_steps
        assert rows * n_steps == r and rows % 16 == 0, (w.shape, n_steps)
        in_specs.append(pl.BlockSpec((None, rows, c),
                                     lambda *g, layer=layer: (layer, step_of(*g), 0)))
        out_specs.append(pl.BlockSpec((rows, c), lambda *g: (step_of(*g), 0)))
        out_shape.append(jax.ShapeDtypeStruct((r, c), BF16))
    return in_specs, out_specs, out_shape


def _run_riders(src_refs, dst_refs):
    for src, dst in zip(src_refs, dst_refs):
        dst[...] = src[...].astype(BF16)


def _qkv_kernel(x_ref, g_ref, w_ref, qg_ref, kg_ref, ones_ref, *rest):
    out_refs, hn_sc = rest[:A_N_GROUPS], rest[A_N_GROUPS]
    hn = _rms(x_ref[...], g_ref[...])
    tm = hn.shape[0]
    for j in range(N_SLABS):
        hn_sc[j] = hn[:, j * LANES:(j + 1) * LANES]
    for grp, (_, dil) in enumerate(A_GROUPS):
        o_ref = out_refs[grp]
        n = tm // dil
        if dil == 1:
            hg = hn.astype(BF16)
        else:
            hg = jnp.concatenate(
                [jnp.concatenate([hn_sc[j, pl.ds(r, n, stride=dil), :] for j in range(N_SLABS)],
                                 axis=1) for r in range(dil)], axis=0).astype(BF16)
        for kind in range(3):
            c = 3 * grp + kind
            y = jnp.dot(hg, w_ref[:, c * D_MODEL:(c + 1) * D_MODEL], preferred_element_type=F32)
            if kind < 2:
                y2 = (y * y).astype(BF16)
                parts = [jnp.dot(y2[:, t * MXU_TILE:(t + 1) * MXU_TILE], ones_ref[...],
                                 preferred_element_type=F32) for t in range(D_MODEL // MXU_TILE)]
                ss = jnp.concatenate(parts, axis=1)
                gain = qg_ref[grp] if kind == 0 else kg_ref[grp]
                y = y * lax.rsqrt(ss * (1.0 / A_HEAD_DIM) + EPS) * gain
            yb = y.astype(BF16)
            ks = slice(kind * D_MODEL, (kind + 1) * D_MODEL)
            for r in range(dil):
                o_ref[r, :, ks] = yb[r * n:(r + 1) * n]


def _qkv_proj(x, g, w, qg, kg, ones):
    s = x.shape[0]
    out_specs, out_shape = [], []
    for _, dil in A_GROUPS:
        out_specs.append(pl.BlockSpec((dil, ROW_TILE // dil, 3 * D_MODEL), lambda i: (0, i, 0)))
        out_shape.append(jax.ShapeDtypeStruct((dil, s // dil, 3 * D_MODEL), BF16))
    return pl.pallas_call(
        _qkv_kernel,
        grid=(s // ROW_TILE,),
        in_specs=[pl.BlockSpec((ROW_TILE, D_MODEL), lambda i: (i, 0)),
                  _resident(g.shape), _resident(w.shape), _resident(qg.shape),
                  _resident(kg.shape), _resident(ones.shape)],
        out_specs=out_specs,
        out_shape=out_shape,
        scratch_shapes=[pltpu.VMEM((N_SLABS, ROW_TILE, LANES), F32)],
        compiler_params=_params(1),
        name="qkv_proj",
    )(x, g, w, qg, kg, ones)


def _attn_kernel(q_ref, kp_ref, kc_ref, vp_ref, vc_ref, bias_ref, hmask_ref, *rest, dil, n_riders):
    o_ref, m_ref, l_ref = rest[n_riders:n_riders + 3]
    _run_riders(rest[:n_riders], rest[n_riders + 3:])
    n_res = q_ref.shape[0]
    n_tiles = q_ref.shape[1] // Q_BLOCK
    lane = lax.broadcasted_iota(jnp.int32, (Q_BLOCK, LANES), 1)
    lo = lane < A_HEAD_DIM
    for sub in range(n_res):
        res = pl.program_id(1) * n_res + sub
        for tile in range(n_tiles):
            cur = slice(tile * Q_BLOCK, (tile + 1) * Q_BLOCK)
            both = slice((tile - 1) * Q_BLOCK, (tile + 1) * Q_BLOCK)
            if dil > 1:
                rows = pl.ds(tile * Q_BLOCK * dil + res, Q_BLOCK, stride=dil)
            else:
                rows = cur
            if tile == 0:
                sel = (pl.program_id(0) == 0).astype(jnp.int32)
            else:
                sel = 0
            m_tile = jnp.zeros((Q_BLOCK, LANES), F32)
            l_tile = jnp.ones((Q_BLOCK, LANES), F32)
            for pair in range(A_HEADS // 2):
                cs = slice(pair * LANES, (pair + 1) * LANES)
                q2 = q_ref[sub, cur, cs]
                if tile == 0:
                    kcat = jnp.concatenate([kp_ref[sub, :, cs], kc_ref[sub, cur, cs]], axis=0)
                    vcat = jnp.concatenate([vp_ref[sub, :, cs], vc_ref[sub, cur, cs]], axis=0)
                else:
                    kcat = kc_ref[sub, both, cs]
                    vcat = vc_ref[sub, both, cs]
                qq = jnp.concatenate([q2 * hmask_ref[0], q2 * hmask_ref[1]], axis=0)
                s = lax.dot_general(qq, kcat, NT_DIMS, preferred_element_type=F32)
                s = s + bias_ref[sel, pair]
                m = jnp.max(s, axis=-1, keepdims=True)
                p = jnp.exp(s - m)
                l = jnp.sum(p, axis=-1, keepdims=True)
                u = jnp.dot(p.astype(BF16), vcat, preferred_element_type=F32)
                for hh in range(2):
                    hr = slice(hh * Q_BLOCK, (hh + 1) * Q_BLOCK)
                    m_tile = jnp.where(lane == 2 * pair + hh, m[hr], m_tile)
                    l_tile = jnp.where(lane == 2 * pair + hh, l[hr], l_tile)
                o_ref[pair, rows, :] = jnp.where(lo, u[:Q_BLOCK], u[Q_BLOCK:])
            m_ref[rows, :] = m_tile
            l_ref[rows, :] = l_tile


def _attn_bias(dil):
    slopes = np.asarray([2.0 ** (-8.0 * (h + 1) / A_HEADS) for h in range(A_HEADS)], np.float32)
    row = np.arange(Q_BLOCK)[:, None]
    col = np.arange(Q_BLOCK)[None, :]

    def table(steps, valid):
        dist = (steps * dil).astype(np.float32)
        return np.where(valid[None], -(slopes[:, None, None] * dist[None]), np.float32(NEG_INF))

    prev = table(Q_BLOCK + row - col, col >= row)
    cur = table(row - col, col <= row)
    normal = np.concatenate([prev, cur], axis=-1)
    firstb = np.concatenate([np.full_like(prev, NEG_INF), cur], axis=-1)
    tables = np.stack([normal, firstb]).astype(np.float32)
    return tables.reshape(2, A_HEADS // 2, 2 * Q_BLOCK, 2 * Q_BLOCK)


def _attention_group(qkv_g, grp, dil, bias, hmask, riders):
    _, sd, _ = qkv_g.shape
    s = sd * dil
    n_res = min(TILES_PER_STEP, dil)
    n_tiles = TILES_PER_STEP // n_res
    blk = n_tiles * Q_BLOCK

    def col_spec(off, prev):
        if prev:
            return pl.BlockSpec((n_res, Q_BLOCK, D_MODEL),
                                lambda b, r: (r, jnp.maximum(b * n_tiles - 1, 0), off))
        return pl.BlockSpec((n_res, blk, D_MODEL), lambda b, r: (r, b, off))

    grid = (sd // blk, dil // n_res)
    r_in, r_out, r_shape = _rider_specs(riders, grid[0] * grid[1], lambda b, r: b * grid[1] + r)
    outs = pl.pallas_call(
        functools.partial(_attn_kernel, dil=dil, n_riders=len(riders)),
        grid=grid,
        in_specs=[col_spec(0, False), col_spec(1, True), col_spec(1, False),
                  col_spec(2, True), col_spec(2, False),
                  _resident(bias.shape), _resident(hmask.shape), *r_in],
        out_specs=[pl.BlockSpec((N_SLABS, blk * dil, LANES), lambda b, r: (0, b, 0)),
                   pl.BlockSpec((blk * dil, LANES), lambda b, r: (b, 0)),
                   pl.BlockSpec((blk * dil, LANES), lambda b, r: (b, 0)), *r_out],
        out_shape=[jax.ShapeDtypeStruct((N_SLABS, s, LANES), F32),
                   jax.ShapeDtypeStruct((s, LANES), F32), jax.ShapeDtypeStruct((s, LANES), F32),
                   *r_shape],
        compiler_params=_params(2),
        name=f"dilated_attn_g{grp}",
    )(qkv_g, qkv_g, qkv_g, qkv_g, qkv_g, bias, hmask, *[w for w, _ in riders])
    return outs[:3], outs[3:]


def _merge_groups(u_refs, m_refs, l_refs, expand_ref):
    ms = [r[...] for r in m_refs]
    top = functools.reduce(jnp.maximum, ms)
    es = [jnp.exp(v - top) for v in ms]
    den = functools.reduce(jnp.add, [e * r[...] for e, r in zip(es, l_refs)])
    merged = None
    for u_ref, e in zip(u_refs, es):
        wide = jnp.dot((e / den).astype(BF16), expand_ref[...], preferred_element_type=F32)
        u = jnp.concatenate([u_ref[j] for j in range(N_SLABS)], axis=1)
        merged = wide * u if merged is None else merged + wide * u
    return merged.astype(BF16)


def _post_body(x, mix, wo_ref, gm_ref, w1_ref, w2_ref, gp_ref, wg_ref, p, wp_ref):
    x = x + jnp.dot(mix, wo_ref[...], preferred_element_type=F32)
    hn = _rms(x, gm_ref[...]).astype(BF16)
    acc = jnp.zeros_like(x)
    for c in range(D_FF // D_MODEL):
        cs = slice(c * D_MODEL, (c + 1) * D_MODEL)
        a = jnp.dot(hn, w1_ref[:, cs], preferred_element_type=F32)
        a = jnp.square(jnp.maximum(a, 0.0)).astype(BF16)
        acc = acc + jnp.dot(a, w2_ref[cs, :], preferred_element_type=F32)
    x = x + acc
    gate = jax.nn.sigmoid(jnp.dot(_rms(x, gp_ref[...]).astype(BF16), wg_ref[...],
                                  preferred_element_type=F32))
    emb = jnp.dot(p.astype(BF16), wp_ref[...], preferred_element_type=F32)
    return x + gate * emb


def _post_kernel(x_ref, *refs):
    g = A_N_GROUPS
    mix = _merge_groups(refs[:g], refs[g:2 * g], refs[2 * g:3 * g], refs[3 * g])
    wo_ref, gm_ref, w1_ref, w2_ref, gp_ref, wg_ref, p_ref, wp_ref, o_ref = refs[3 * g + 1:]
    o_ref[...] = _post_body(x_ref[...], mix, wo_ref, gm_ref, w1_ref, w2_ref, gp_ref, wg_ref,
                            p_ref[...], wp_ref)


def _post_mixer(x, sums, maxs, dens, expand, wo, gm, w1, w2, gp, wg, p, layer, wp):
    s = x.shape[0]
    row = lambda width: pl.BlockSpec((ROW_TILE, width), lambda i: (i, 0))
    mix_specs = ([pl.BlockSpec((N_SLABS, ROW_TILE, LANES), lambda i: (0, i, 0))] * len(sums)
                 + [row(LANES)] * (len(maxs) + len(dens)) + [_resident(expand.shape)])
    return pl.pallas_call(
        _post_kernel,
        grid=(s // ROW_TILE,),
        in_specs=[row(D_MODEL), *mix_specs, _resident(wo.shape), _resident(gm.shape),
                  _resident(w1.shape), _resident(w2.shape), _resident(gp.shape),
                  _resident(wg.shape),
                  pl.BlockSpec((None, ROW_TILE, PLE_DIM), lambda i: (layer, i, 0)),
                  _resident(wp.shape)],
        out_specs=row(D_MODEL),
        out_shape=jax.ShapeDtypeStruct((s, D_MODEL), F32),
        compiler_params=_params(1),
        name="post_mixer",
    )(x, *sums, *maxs, *dens, expand, wo, gm, w1, w2, gp, wg, p, wp)


HALO = 8


def _up_conv_kernel(x_ref, g_ref, wup_ref, cw_ref, cb_ref, sk_ref, wq_ref, wk_ref, wv_ref, wg_ref,
                    bg_ref, q_ref, k_ref, v_ref, skx_ref, zg_ref, gt_ref, halo_sc, ext_sc):
    @pl.when(pl.program_id(0) == 0)
    def _():
        halo_sc[...] = jnp.zeros_like(halo_sc)

    hn = _rms(x_ref[...], g_ref[...]).astype(BF16)
    tm = hn.shape[0]
    for c in range(B_INNER // D_MODEL):
        cs = slice(c * D_MODEL, (c + 1) * D_MODEL)
        z = jnp.dot(hn, wup_ref[:, B_INNER + c * D_MODEL:B_INNER + (c + 1) * D_MODEL],
                    preferred_element_type=F32)
        zg_ref[:, cs] = jax.nn.silu(z)
    gates = jnp.broadcast_to(bg_ref[...], gt_ref.shape)
    for t in range(B_INNER // MXU_TILE):
        ts = slice(t * MXU_TILE, (t + 1) * MXU_TILE)
        xm = jnp.dot(hn, wup_ref[:, ts], preferred_element_type=F32)
        ext_sc[:HALO, :] = halo_sc[:, ts]
        ext_sc[HALO:, :] = xm
        halo_sc[:, ts] = xm[tm - HALO:]
        y = cb_ref[:, ts] + cw_ref[B_CONV - 1:B_CONV, ts] * xm
        for j in range(1, B_CONV):
            y = y + cw_ref[B_CONV - 1 - j:B_CONV - j, ts] * ext_sc[pl.ds(HALO - j, tm), :]
        xc = jax.nn.silu(y)
        skx_ref[:, ts] = sk_ref[:, ts] * xc
        xcb = xc.astype(BF16)
        qf = jnp.dot(xcb, wq_ref[t], preferred_element_type=F32)
        kf = jnp.dot(xcb, wk_ref[t], preferred_element_type=F32)
        vf = jnp.dot(xm.astype(BF16), wv_ref[t], preferred_element_type=F32)
        qb, kb, vb = qf.astype(BF16), kf.astype(BF16), vf.astype(BF16)
        q_ref[:, ts] = qb
        k_ref[:, ts] = (kf * B_HEAD_DIM ** -0.5).astype(BF16)
        v_ref[:, ts] = vb
        for part, val in enumerate((qb, kb, vb)):
            ws = slice(part * B_INNER + t * MXU_TILE, part * B_INNER + (t + 1) * MXU_TILE)
            gates = gates + jnp.dot(val, wg_ref[ws, :], preferred_element_type=F32)
    gt_ref[...] = gates


def _up_conv(x, g, wup, cw, cb, sk, wq, wk, wv, wg, bg):
    s = x.shape[0]
    row = lambda: pl.BlockSpec((ROW_TILE, B_INNER), lambda i: (i, 0))
    return pl.pallas_call(
        _up_conv_kernel,
        grid=(s // ROW_TILE,),
        in_specs=[pl.BlockSpec((ROW_TILE, D_MODEL), lambda i: (i, 0)),
                  _resident(g.shape), _resident(wup.shape),
                  _resident(cw.shape), _resident(cb.shape), _resident(sk.shape),
                  _resident(wq.shape), _resident(wk.shape), _resident(wv.shape),
                  _resident(wg.shape), _resident(bg.shape)],
        out_specs=[row(), row(), row(), row(), row(),
                   pl.BlockSpec((ROW_TILE, LANES), lambda i: (i, 0))],
        out_shape=[jax.ShapeDtypeStruct((s, B_INNER), BF16)] * 3
        + [jax.ShapeDtypeStruct((s, B_INNER), F32)] * 2 + [jax.ShapeDtypeStruct((s, LANES), F32)],
        scratch_shapes=[pltpu.VMEM((HALO, B_INNER), F32),
                        pltpu.VMEM((HALO + ROW_TILE, MXU_TILE), F32)],
        compiler_params=_params(1),
        name="up_conv_qkv_gates",
    )(x, g, wup, cw, cb, sk, wq, wk, wv, wg, bg)


N_COLS = B_HEAD_DIM + LANES


def _mlstm_chunk(q_ref, k_ref, v_ref, g_ref, skx_ref, zg_ref, tri_ref, hg_ref, o_ref,
                 c_sc, cb_sc, m_sc):
    chunk = q_ref.shape[0]
    g_all = g_ref[...]
    lf_all = jax.nn.log_sigmoid(g_all)
    b_all = jnp.dot(tri_ref[...], lf_all, preferred_element_type=F32,
                    precision=lax.Precision.HIGHEST)
    b_t = b_all.T
    g_t = g_all.T
    row = lax.broadcasted_iota(jnp.int32, (chunk, chunk), 0)
    col = lax.broadcasted_iota(jnp.int32, (chunk, chunk), 1)
    causal = col <= row
    ones = jnp.ones((chunk, LANES), BF16)
    for h in range(B_HEADS):
        hs = slice(h * B_HEAD_DIM, (h + 1) * B_HEAD_DIM)
        i_col = g_all[:, h:h + 1]
        b_col = b_all[:, B_HEADS + h:B_HEADS + h + 1]
        i_row = g_t[h:h + 1, :]
        b_row = b_t[B_HEADS + h:B_HEADS + h + 1, :]
        m_prev = m_sc[h, 0:1, 0:1]
        dmat = jnp.where(causal, b_col - b_row + i_row, NEG_INF)
        inter = b_col + m_prev
        m_t = jnp.maximum(inter, jnp.max(dmat, axis=-1, keepdims=True))
        qh = q_ref[:, hs]
        kh = k_ref[:, hs]
        vh = v_ref[:, hs]
        s = lax.dot_general(qh, kh, NT_DIMS, preferred_element_type=F32) * jnp.exp(dmat - m_t)
        sc = jnp.exp(inter - m_t)
        qc = jnp.dot(qh, cb_sc[h], preferred_element_type=F32)
        num = sc * qc[:, :B_HEAD_DIM] + jnp.dot(s.astype(BF16), vh, preferred_element_type=F32)
        den = sc * qc[:, B_HEAD_DIM:B_HEAD_DIM + 1] + jnp.sum(s, axis=-1, keepdims=True)
        hv = num / jnp.maximum(jnp.abs(den), jnp.exp(-m_t))
        b_last = b_col[chunk - 1:chunk, :]
        g_col = b_last - b_col + i_col
        m_new = jnp.maximum(b_last + m_prev, jnp.max(g_col, axis=0, keepdims=True))
        decay = jnp.exp(b_last + m_prev - m_new)
        wkb = (kh.astype(F32) * jnp.exp(g_col - m_new)).astype(BF16)
        vx = jnp.concatenate([vh, ones], axis=1)
        for rb in range(B_HEAD_DIM // LANES):
            rs = slice(rb * LANES, (rb + 1) * LANES)
            c_new = decay * c_sc[h, rs, :] + lax.dot_general(wkb[:, rs], vx, TN_DIMS,
                                                             preferred_element_type=F32)
            c_sc[h, rs, :] = c_new
            cb_sc[h, rs, :] = c_new.astype(BF16)
        m_sc[h] = jnp.broadcast_to(m_new, m_sc.shape[1:])
        o_ref[:, hs] = ((_rms(hv, hg_ref[:, hs]) + skx_ref[:, hs]) * zg_ref[:, hs]).astype(BF16)


def _scan_post_kernel(q_ref, k_ref, v_ref, g_ref, skx_ref, zg_ref, tri_ref, hg_ref,
                      x_ref, wo_ref, gm_ref, w1_ref, w2_ref, gp_ref, wg_ref, p_ref, wp_ref,
                      o_ref, c_sc, cb_sc, m_sc, mix_sc):
    @pl.when(pl.program_id(0) == 0)
    def _():
        c_sc[...] = jnp.zeros_like(c_sc)
        cb_sc[...] = jnp.zeros_like(cb_sc)
        m_sc[...] = jnp.full_like(m_sc, NEG_INF)
        mix_sc[...] = jnp.zeros_like(mix_sc)

    o_ref[...] = _post_body(x_ref[...], mix_sc[...], wo_ref, gm_ref, w1_ref, w2_ref, gp_ref,
                            wg_ref, p_ref[...], wp_ref)
    _mlstm_chunk(q_ref, k_ref, v_ref, g_ref, skx_ref, zg_ref, tri_ref, hg_ref, mix_sc,
                 c_sc, cb_sc, m_sc)


def _scan_post(q, k, v, gates, skx, zg, tri, hg, x, wo, gm, w1, w2, gp, wg, p, layer, wp):
    s = q.shape[0]
    n = s // SCAN_CHUNK
    ahead = lambda width: pl.BlockSpec((SCAN_CHUNK, width), lambda c: (jnp.minimum(c, n - 1), 0))
    behind = lambda width: pl.BlockSpec((SCAN_CHUNK, width), lambda c: (jnp.maximum(c - 1, 0), 0))
    return pl.pallas_call(
        _scan_post_kernel,
        grid=(n + 1,),
        in_specs=[ahead(B_INNER), ahead(B_INNER), ahead(B_INNER), ahead(LANES), ahead(B_INNER),
                  ahead(B_INNER), _resident(tri.shape), _resident(hg.shape),
                  behind(D_MODEL), _resident(wo.shape), _resident(gm.shape), _resident(w1.shape),
                  _resident(w2.shape), _resident(gp.shape), _resident(wg.shape),
                  pl.BlockSpec((None, SCAN_CHUNK, PLE_DIM),
                               lambda c: (layer, jnp.maximum(c - 1, 0), 0)),
                  _resident(wp.shape)],
        out_specs=behind(D_MODEL),
        out_shape=jax.ShapeDtypeStruct((s, D_MODEL), F32),
        scratch_shapes=[pltpu.VMEM((B_HEADS, B_HEAD_DIM, N_COLS), F32),
                        pltpu.VMEM((B_HEADS, B_HEAD_DIM, N_COLS), BF16),
                        pltpu.VMEM((B_HEADS, 8, LANES), F32),
                        pltpu.VMEM((SCAN_CHUNK, B_INNER), BF16)],
        compiler_params=_params(1),
        name="mlstm_scan_post_mixer",
    )(q, k, v, gates, skx, zg, tri, hg, x, wo, gm, w1, w2, gp, wg, p, wp)


def _block_diag_tiles(w):
    nblk, blk, _ = w.shape
    rows = w.reshape(nblk * blk // MXU_TILE, MXU_TILE, blk)
    idx = jnp.arange(MXU_TILE) // blk
    same_block = (idx[:, None] == idx[None, :]).astype(w.dtype)
    return (jnp.tile(rows, (1, 1, MXU_TILE // blk)) * same_block[None]).astype(BF16)


def _row(v):
    return v.reshape(1, -1).astype(F32)


def kernel(x, p, a_norm, a_w_qkv, a_q_gain, a_k_gain, a_w_o, b_norm, b_w_up, b_conv_w, b_conv_b,
           b_w_q, b_w_k, b_w_v, b_w_gate, b_b_gate, b_h_gain, b_skip, b_w_down, mlp_norm, mlp_w1,
           mlp_w2, ple_norm, ple_w_gate, ple_w_proj):
    bsz, seq, _ = x.shape
    assert bsz == 1
    xs = x.reshape(seq, D_MODEL)
    ps = p.reshape(p.shape[0], seq, PLE_DIM)

    qg = jnp.tile(a_q_gain[0], (1, A_HEADS)).reshape(A_N_GROUPS, 1, D_MODEL) * A_HEAD_DIM ** -0.5
    kg = jnp.tile(a_k_gain[0], (1, A_HEADS)).reshape(A_N_GROUPS, 1, D_MODEL)
    seg = np.arange(MXU_TILE) // A_HEAD_DIM
    ones = (seg[:, None] == seg[None, :]).astype(BF16)
    low = np.broadcast_to(np.arange(LANES) < A_HEAD_DIM, (Q_BLOCK, LANES))
    hmask = np.stack([low, ~low]).astype(BF16)
    expand = (np.arange(LANES)[:, None] == np.arange(D_MODEL)[None, :] // A_HEAD_DIM).astype(BF16)
    qkv_groups = _qkv_proj(xs, _row(a_norm[0]), a_w_qkv[0].astype(BF16), qg, kg, ones)
    riders = ([(mlp_w1, 0), (a_w_o, 0), (mlp_w1, 1)],
              [(mlp_w2, 0), (ple_w_gate, 0), (ple_w_proj, 0), (mlp_w2, 1)],
              [(b_w_up, 0), (ple_w_proj, 1), (b_w_down, 0), (ple_w_gate, 1)])
    stats, cast = zip(*[_attention_group(qkv_groups[grp], grp, dil, _attn_bias(dil), hmask,
                                         riders[grp]) for grp, (_, dil) in enumerate(A_GROUPS)])
    sums, maxs, dens = zip(*stats)
    (w1_0, wo_0, w1_1), (w2_0, wg_0, wp_0, w2_1), (wup, wp_1, wo_1, wg_1) = cast
    xs = _post_mixer(xs, sums, maxs, dens, expand, wo_0, _row(mlp_norm[0]), w1_0, w2_0,
                     _row(ple_norm[0]), wg_0, ps, 0, wp_0)

    wg = jnp.zeros((3 * B_INNER, LANES), F32).at[:, :2 * B_HEADS].set(b_w_gate[0]).astype(BF16)
    bg = jnp.zeros((1, LANES), F32).at[0, :2 * B_HEADS].set(b_b_gate[0])
    q, k, v, skx, zg, gates = _up_conv(xs, _row(b_norm[0]), wup, b_conv_w[0],
                                       _row(b_conv_b[0]), _row(b_skip[0]),
                                       _block_diag_tiles(b_w_q[0]), _block_diag_tiles(b_w_k[0]),
                                       _block_diag_tiles(b_w_v[0]), wg, bg)
    tri = (np.arange(SCAN_CHUNK)[:, None] >= np.arange(SCAN_CHUNK)[None, :]).astype(np.float32)
    xs = _scan_post(q, k, v, gates, skx, zg, tri, _row(b_h_gain[0]), xs, wo_1, _row(mlp_norm[1]),
                    w1_1, w2_1, _row(ple_norm[1]), wg_1, ps, 1, wp_1)
    return xs.reshape(bsz, seq, D_MODEL)
```

```python
import functools

import jax
import jax.numpy as jnp
import numpy as np
from jax import lax
from jax.experimental import pallas as pl
from jax.experimental.pallas import tpu as pltpu

F32 = jnp.float32
BF16 = jnp.bfloat16

D_MODEL = 1024
A_HEADS = 16
A_HEAD_DIM = D_MODEL // A_HEADS
A_GROUPS = ((128, 1), (512, 4), (2048, 16))
A_N_GROUPS = len(A_GROUPS)
Q_BLOCK = 128
TILES_PER_STEP = 8
MAX_STORE_STRIDE = 4
B_INNER = 2 * D_MODEL
B_HEADS = 4
B_HEAD_DIM = B_INNER // B_HEADS
B_CONV = 4
SCAN_CHUNK = 256
D_FF = 4 * D_MODEL
PLE_DIM = 256
EPS = 1e-6
NEG_INF = -1e30

MXU_TILE = 256
LANES = 128
SUBLANES = 8
BF16_ROWS = 16
N_SLABS = D_MODEL // LANES
VMEM_LIMIT_BYTES = 60000 * 1024
ROW_TILE = 512

NT_DIMS = (((1,), (1,)), ((), ()))
TN_DIMS = (((0,), (0,)), ((), ()))


def _params(n_axes):
    return pltpu.CompilerParams(dimension_semantics=("arbitrary",) * n_axes,
                                vmem_limit_bytes=VMEM_LIMIT_BYTES)


def _resident(shape):
    return pl.BlockSpec(shape, lambda *_: (0,) * len(shape), pipeline_mode=pl.Buffered(1))


def _rms(x, g):
    ms = jnp.mean(x * x, axis=-1, keepdims=True)
    return x * lax.rsqrt(ms + EPS) * g


def _rider_specs(jobs, n_steps, step_of):
    in_specs, out_specs, out_shape = [], [], []
    for w, layer in jobs:
        _, r, c = w.shape
        rows = r // n_steps
        assert rows * n_steps == r and rows % BF16_ROWS == 0, (w.shape, n_steps)
        in_specs.append(pl.BlockSpec((None, rows, c),
                                     lambda *g, layer=layer: (layer, step_of(*g), 0)))
        out_specs.append(pl.BlockSpec((rows, c), lambda *g: (step_of(*g), 0)))
        out_shape.append(jax.ShapeDtypeStruct((r, c), BF16))
    return in_specs, out_specs, out_shape


def _run_riders(src_refs, dst_refs):
    for src, dst in zip(src_refs, dst_refs):
        dst[...] = src[...].astype(BF16)


def _qkv_kernel(x_ref, g_ref, w_ref, qg_ref, kg_ref, ones_ref, *rest):
    out_refs, hn_sc = rest[:A_N_GROUPS], rest[A_N_GROUPS]
    hn = _rms(x_ref[...], g_ref[...])
    tm = hn.shape[0]
    for j in range(N_SLABS):
        hn_sc[j] = hn[:, j * LANES:(j + 1) * LANES]
    for grp, (_, dil) in enumerate(A_GROUPS):
        o_ref = out_refs[grp]
        n = tm // dil
        if dil == 1:
            hg = hn.astype(BF16)
        else:
            hg = jnp.concatenate(
                [jnp.concatenate([hn_sc[j, pl.ds(r, n, stride=dil), :] for j in range(N_SLABS)],
                                 axis=1) for r in range(dil)], axis=0).astype(BF16)
        for kind in range(3):
            c = 3 * grp + kind
            y = jnp.dot(hg, w_ref[:, c * D_MODEL:(c + 1) * D_MODEL], preferred_element_type=F32)
            if kind < 2:
                y2 = (y * y).astype(BF16)
                parts = [jnp.dot(y2[:, t * MXU_TILE:(t + 1) * MXU_TILE], ones_ref[...],
                                 preferred_element_type=F32) for t in range(D_MODEL // MXU_TILE)]
                ss = jnp.concatenate(parts, axis=1)
                gain = qg_ref[grp] if kind == 0 else kg_ref[grp]
                y = y * lax.rsqrt(ss * (1.0 / A_HEAD_DIM) + EPS) * gain
            yb = y.astype(BF16)
            ks = slice(kind * D_MODEL, (kind + 1) * D_MODEL)
            for r in range(dil):
                o_ref[r, :, ks] = yb[r * n:(r + 1) * n]


def _qkv_proj(x, g, w, qg, kg, ones):
    s = x.shape[0]
    out_specs, out_shape = [], []
    for _, dil in A_GROUPS:
        out_specs.append(pl.BlockSpec((dil, ROW_TILE // dil, 3 * D_MODEL), lambda i: (0, i, 0)))
        out_shape.append(jax.ShapeDtypeStruct((dil, s // dil, 3 * D_MODEL), BF16))
    return pl.pallas_call(
        _qkv_kernel,
        grid=(s // ROW_TILE,),
        in_specs=[pl.BlockSpec((ROW_TILE, D_MODEL), lambda i: (i, 0)),
                  _resident(g.shape), _resident(w.shape), _resident(qg.shape),
                  _resident(kg.shape), _resident(ones.shape)],
        out_specs=out_specs,
        out_shape=out_shape,
        scratch_shapes=[pltpu.VMEM((N_SLABS, ROW_TILE, LANES), F32)],
        compiler_params=_params(1),
        name="qkv_proj",
    )(x, g, w, qg, kg, ones)


def _attn_kernel(q_ref, kp_ref, kc_ref, vp_ref, vc_ref, bias_ref, hmask_ref, *rest, dil, n_riders):
    o_ref, m_ref, l_ref = rest[n_riders:n_riders + 3]
    _run_riders(rest[:n_riders], rest[n_riders + 3:])
    n_res = q_ref.shape[0]
    n_tiles = q_ref.shape[1] // Q_BLOCK
    planes = m_ref.shape[0]
    stride = dil // planes
    lane = lax.broadcasted_iota(jnp.int32, (Q_BLOCK, LANES), 1)
    lo = lane < A_HEAD_DIM
    for sub in range(n_res):
        plane = sub % planes
        pos = pl.program_id(1) * (n_res // planes) + sub // planes
        for tile in range(n_tiles):
            cur = slice(tile * Q_BLOCK, (tile + 1) * Q_BLOCK)
            both = slice((tile - 1) * Q_BLOCK, (tile + 1) * Q_BLOCK)
            if stride > 1:
                rows = pl.ds(tile * Q_BLOCK * stride + pos, Q_BLOCK, stride=stride)
            else:
                rows = cur
            if tile == 0:
                sel = (pl.program_id(0) == 0).astype(jnp.int32)
            else:
                sel = 0
            m_tile = jnp.zeros((Q_BLOCK, LANES), F32)
            l_tile = jnp.ones((Q_BLOCK, LANES), F32)
            for pair in range(A_HEADS // 2):
                cs = slice(pair * LANES, (pair + 1) * LANES)
                q2 = q_ref[sub, cur, cs]
                if tile == 0:
                    kcat = jnp.concatenate([kp_ref[sub, :, cs], kc_ref[sub, cur, cs]], axis=0)
                    vcat = jnp.concatenate([vp_ref[sub, :, cs], vc_ref[sub, cur, cs]], axis=0)
                else:
                    kcat = kc_ref[sub, both, cs]
                    vcat = vc_ref[sub, both, cs]
                qq = jnp.concatenate([q2 * hmask_ref[0], q2 * hmask_ref[1]], axis=0)
                s = lax.dot_general(qq, kcat, NT_DIMS, preferred_element_type=F32)
                s = s + bias_ref[sel, pair]
                m = jnp.max(s, axis=-1, keepdims=True)
                p = jnp.exp(s - m)
                l = jnp.sum(p, axis=-1, keepdims=True)
                u = jnp.dot(p.astype(BF16), vcat, preferred_element_type=F32)
                for hh in range(2):
                    hr = slice(hh * Q_BLOCK, (hh + 1) * Q_BLOCK)
                    m_tile = jnp.where(lane == 2 * pair + hh, m[hr], m_tile)
                    l_tile = jnp.where(lane == 2 * pair + hh, l[hr], l_tile)
                o_ref[pair, plane, rows, :] = jnp.where(lo, u[:Q_BLOCK], u[Q_BLOCK:])
            m_ref[plane, rows, :] = m_tile
            l_ref[plane, rows, :] = l_tile


def _attn_bias(dil):
    slopes = np.asarray([2.0 ** (-8.0 * (h + 1) / A_HEADS) for h in range(A_HEADS)], np.float32)
    row = np.arange(Q_BLOCK)[:, None]
    col = np.arange(Q_BLOCK)[None, :]

    def table(steps, valid):
        dist = (steps * dil).astype(np.float32)
        return np.where(valid[None], -(slopes[:, None, None] * dist[None]), np.float32(NEG_INF))

    prev = table(Q_BLOCK + row - col, col >= row)
    cur = table(row - col, col <= row)
    normal = np.concatenate([prev, cur], axis=-1)
    firstb = np.concatenate([np.full_like(prev, NEG_INF), cur], axis=-1)
    tables = np.stack([normal, firstb]).astype(np.float32)
    return tables.reshape(2, A_HEADS // 2, 2 * Q_BLOCK, 2 * Q_BLOCK)


def _attention_group(qkv_g, grp, dil, bias, hmask, riders):
    _, sd, _ = qkv_g.shape
    s = sd * dil
    n_res = min(TILES_PER_STEP, dil)
    n_tiles = TILES_PER_STEP // n_res
    blk = n_tiles * Q_BLOCK
    planes = max(1, dil // MAX_STORE_STRIDE)
    assert n_res % planes == 0
    prow = blk * dil // planes

    def col_spec(off, prev):
        if prev:
            return pl.BlockSpec((n_res, Q_BLOCK, D_MODEL),
                                lambda b, r: (r, jnp.maximum(b * n_tiles - 1, 0), off))
        return pl.BlockSpec((n_res, blk, D_MODEL), lambda b, r: (r, b, off))

    grid = (sd // blk, dil // n_res)
    r_in, r_out, r_shape = _rider_specs(riders, grid[0] * grid[1], lambda b, r: b * grid[1] + r)
    outs = pl.pallas_call(
        functools.partial(_attn_kernel, dil=dil, n_riders=len(riders)),
        grid=grid,
        in_specs=[col_spec(0, False), col_spec(1, True), col_spec(1, False),
                  col_spec(2, True), col_spec(2, False),
                  _resident(bias.shape), _resident(hmask.shape), *r_in],
        out_specs=[pl.BlockSpec((N_SLABS, planes, prow, LANES), lambda b, r: (0, 0, b, 0)),
                   pl.BlockSpec((planes, prow, LANES), lambda b, r: (0, b, 0)),
                   pl.BlockSpec((planes, prow, LANES), lambda b, r: (0, b, 0)), *r_out],
        out_shape=[jax.ShapeDtypeStruct((N_SLABS, planes, s // planes, LANES), F32),
                   jax.ShapeDtypeStruct((planes, s // planes, LANES), F32),
                   jax.ShapeDtypeStruct((planes, s // planes, LANES), F32), *r_shape],
        compiler_params=_params(2),
        name=f"dilated_attn_g{grp}",
    )(qkv_g, qkv_g, qkv_g, qkv_g, qkv_g, bias, hmask, *[w for w, _ in riders])
    return outs[:3], outs[3:]


def _natural_rows(ref, lead, stage_ref):
    planes = ref.shape[-3]
    if planes == 1:
        return ref[(*lead, 0)]
    for b in range(planes):
        stage_ref[pl.ds(b, ref.shape[-2], stride=planes), :] = ref[(*lead, b)]
    return stage_ref[...]


def _merge_groups(u_refs, m_refs, l_refs, expand_ref, stage_ref):
    ms = [_natural_rows(r, (), stage_ref.at[N_SLABS]) for r in m_refs]
    ls = [_natural_rows(r, (), stage_ref.at[N_SLABS + 1]) for r in l_refs]
    top = functools.reduce(jnp.maximum, ms)
    es = [jnp.exp(v - top) for v in ms]
    den = functools.reduce(jnp.add, [e * l for e, l in zip(es, ls)])
    merged = None
    for u_ref, e in zip(u_refs, es):
        wide = jnp.dot((e / den).astype(BF16), expand_ref[...], preferred_element_type=F32)
        u = jnp.concatenate([_natural_rows(u_ref, (j,), stage_ref.at[j]) for j in range(N_SLABS)],
                            axis=1)
        merged = wide * u if merged is None else merged + wide * u
    return merged.astype(BF16)


def _post_kernel(*refs, merge):
    x_ref = refs[0]
    if merge:
        g = A_N_GROUPS
        mix = _merge_groups(refs[1:1 + g], refs[1 + g:1 + 2 * g], refs[1 + 2 * g:1 + 3 * g],
                            refs[1 + 3 * g], refs[-1])
        refs = refs[2 + 3 * g:-1]
    else:
        mix = refs[1][...]
        refs = refs[2:]
    wo_ref, gm_ref, w1_ref, w2_ref, gp_ref, wg_ref, p_ref, wp_ref, o_ref = refs
    x = x_ref[...] + jnp.dot(mix, wo_ref[...], preferred_element_type=F32)
    hn = _rms(x, gm_ref[...]).astype(BF16)
    acc = jnp.zeros_like(x)
    for c in range(D_FF // D_MODEL):
        cs = slice(c * D_MODEL, (c + 1) * D_MODEL)
        a = jnp.dot(hn, w1_ref[:, cs], preferred_element_type=F32)
        a = jnp.square(jnp.maximum(a, 0.0)).astype(BF16)
        acc = acc + jnp.dot(a, w2_ref[cs, :], preferred_element_type=F32)
    x = x + acc
    gate = jax.nn.sigmoid(jnp.dot(_rms(x, gp_ref[...]).astype(BF16), wg_ref[...],
                                  preferred_element_type=F32))
    emb = jnp.dot(p_ref[...].astype(BF16), wp_ref[...], preferred_element_type=F32)
    o_ref[...] = x + gate * emb


def _post_mixer(x, mix, wo, gm, w1, w2, gp, wg, p, layer, wp):
    s = x.shape[0]
    row = lambda width: pl.BlockSpec((ROW_TILE, width), lambda i: (i, 0))
    merge = isinstance(mix, tuple)
    if merge:
        sums, maxs, dens, expand = mix
        mix_args = [*sums, *maxs, *dens, expand]
        mix_specs = [pl.BlockSpec((N_SLABS, u.shape[1], ROW_TILE // u.shape[1], LANES),
                                  lambda i: (0, 0, i, 0)) for u in sums]
        mix_specs += [pl.BlockSpec((v.shape[0], ROW_TILE // v.shape[0], LANES), lambda i: (0, i, 0))
                      for v in (*maxs, *dens)]
        mix_specs.append(_resident(expand.shape))
        scratch = [pltpu.VMEM((N_SLABS + 2, ROW_TILE, LANES), F32)]
    else:
        mix_args = [mix]
        mix_specs = [row(mix.shape[1])]
        scratch = []
    return pl.pallas_call(
        functools.partial(_post_kernel, merge=merge),
        grid=(s // ROW_TILE,),
        in_specs=[row(D_MODEL), *mix_specs, _resident(wo.shape), _resident(gm.shape),
                  _resident(w1.shape), _resident(w2.shape), _resident(gp.shape),
                  _resident(wg.shape),
                  pl.BlockSpec((None, ROW_TILE, PLE_DIM), lambda i: (layer, i, 0)),
                  _resident(wp.shape)],
        out_specs=row(D_MODEL),
        out_shape=jax.ShapeDtypeStruct((s, D_MODEL), F32),
        scratch_shapes=scratch,
        compiler_params=_params(1),
        name="post_mixer",
    )(x, *mix_args, wo, gm, w1, w2, gp, wg, p, wp)


HALO = SUBLANES


def _up_conv_kernel(x_ref, g_ref, wup_ref, cw_ref, cb_ref, sk_ref, wq_ref, wk_ref, wv_ref, wg_ref,
                    bg_ref, q_ref, k_ref, v_ref, skx_ref, zg_ref, gt_ref, halo_sc, ext_sc):
    @pl.when(pl.program_id(0) == 0)
    def _():
        halo_sc[...] = jnp.zeros_like(halo_sc)

    hn = _rms(x_ref[...], g_ref[...]).astype(BF16)
    tm = hn.shape[0]
    for c in range(B_INNER // D_MODEL):
        cs = slice(c * D_MODEL, (c + 1) * D_MODEL)
        z = jnp.dot(hn, wup_ref[:, B_INNER + c * D_MODEL:B_INNER + (c + 1) * D_MODEL],
                    preferred_element_type=F32)
        zg_ref[:, cs] = jax.nn.silu(z)
    gates = jnp.broadcast_to(bg_ref[...], gt_ref.shape)
    for t in range(B_INNER // MXU_TILE):
        ts = slice(t * MXU_TILE, (t + 1) * MXU_TILE)
        xm = jnp.dot(hn, wup_ref[:, ts], preferred_element_type=F32)
        ext_sc[:HALO, :] = halo_sc[:, ts]
        ext_sc[HALO:, :] = xm
        halo_sc[:, ts] = xm[tm - HALO:]
        y = cb_ref[:, ts] + cw_ref[B_CONV - 1:B_CONV, ts] * xm
        for j in range(1, B_CONV):
            y = y + cw_ref[B_CONV - 1 - j:B_CONV - j, ts] * ext_sc[pl.ds(HALO - j, tm), :]
        xc = jax.nn.silu(y)
        skx_ref[:, ts] = sk_ref[:, ts] * xc
        xcb = xc.astype(BF16)
        qf = jnp.dot(xcb, wq_ref[t], preferred_element_type=F32)
        kf = jnp.dot(xcb, wk_ref[t], preferred_element_type=F32)
        vf = jnp.dot(xm.astype(BF16), wv_ref[t], preferred_element_type=F32)
        qb, kb, vb = qf.astype(BF16), kf.astype(BF16), vf.astype(BF16)
        q_ref[:, ts] = qb
        k_ref[:, ts] = (kf * B_HEAD_DIM ** -0.5).astype(BF16)
        v_ref[:, ts] = vb
        for part, val in enumerate((qb, kb, vb)):
            ws = slice(part * B_INNER + t * MXU_TILE, part * B_INNER + (t + 1) * MXU_TILE)
            gates = gates + jnp.dot(val, wg_ref[ws, :], preferred_element_type=F32)
    gt_ref[...] = gates


def _up_conv(x, g, wup, cw, cb, sk, wq, wk, wv, wg, bg):
    s = x.shape[0]
    row = lambda: pl.BlockSpec((ROW_TILE, B_INNER), lambda i: (i, 0))
    return pl.pallas_call(
        _up_conv_kernel,
        grid=(s // ROW_TILE,),
        in_specs=[pl.BlockSpec((ROW_TILE, D_MODEL), lambda i: (i, 0)),
                  _resident(g.shape), _resident(wup.shape),
                  _resident(cw.shape), _resident(cb.shape), _resident(sk.shape),
                  _resident(wq.shape), _resident(wk.shape), _resident(wv.shape),
                  _resident(wg.shape), _resident(bg.shape)],
        out_specs=[row(), row(), row(), row(), row(),
                   pl.BlockSpec((ROW_TILE, LANES), lambda i: (i, 0))],
        out_shape=[jax.ShapeDtypeStruct((s, B_INNER), BF16)] * 3
        + [jax.ShapeDtypeStruct((s, B_INNER), F32)] * 2 + [jax.ShapeDtypeStruct((s, LANES), F32)],
        scratch_shapes=[pltpu.VMEM((HALO, B_INNER), F32),
                        pltpu.VMEM((HALO + ROW_TILE, MXU_TILE), F32)],
        compiler_params=_params(1),
        name="up_conv_qkv_gates",
    )(x, g, wup, cw, cb, sk, wq, wk, wv, wg, bg)


N_COLS = B_HEAD_DIM + LANES


def _mlstm_kernel(q_ref, k_ref, v_ref, g_ref, skx_ref, zg_ref, tri_ref, hg_ref, *rest, n_riders):
    o_ref = rest[n_riders]
    c_sc, cb_sc, m_sc = rest[2 * n_riders + 1:]
    _run_riders(rest[:n_riders], rest[n_riders + 1:2 * n_riders + 1])

    @pl.when(pl.program_id(0) == 0)
    def _():
        c_sc[...] = jnp.zeros_like(c_sc)
        cb_sc[...] = jnp.zeros_like(cb_sc)
        m_sc[...] = jnp.full_like(m_sc, NEG_INF)

    chunk = q_ref.shape[0]
    g_all = g_ref[...]
    lf_all = jax.nn.log_sigmoid(g_all)
    b_all = jnp.dot(tri_ref[...], lf_all, preferred_element_type=F32,
                    precision=lax.Precision.HIGHEST)
    b_t = b_all.T
    g_t = g_all.T
    row = lax.broadcasted_iota(jnp.int32, (chunk, chunk), 0)
    col = lax.broadcasted_iota(jnp.int32, (chunk, chunk), 1)
    causal = col <= row
    ones = jnp.ones((chunk, LANES), BF16)
    for h in range(B_HEADS):
        hs = slice(h * B_HEAD_DIM, (h + 1) * B_HEAD_DIM)
        i_col = g_all[:, h:h + 1]
        b_col = b_all[:, B_HEADS + h:B_HEADS + h + 1]
        i_row = g_t[h:h + 1, :]
        b_row = b_t[B_HEADS + h:B_HEADS + h + 1, :]
        m_prev = m_sc[h, 0:1, 0:1]
        dmat = jnp.where(causal, b_col - b_row + i_row, NEG_INF)
        inter = b_col + m_prev
        m_t = jnp.maximum(inter, jnp.max(dmat, axis=-1, keepdims=True))
        qh = q_ref[:, hs]
        kh = k_ref[:, hs]
        vh = v_ref[:, hs]
        s = lax.dot_general(qh, kh, NT_DIMS, preferred_element_type=F32) * jnp.exp(dmat - m_t)
        sc = jnp.exp(inter - m_t)
        qc = jnp.dot(qh, cb_sc[h], preferred_element_type=F32)
        num = sc * qc[:, :B_HEAD_DIM] + jnp.dot(s.astype(BF16), vh, preferred_element_type=F32)
        den = sc * qc[:, B_HEAD_DIM:B_HEAD_DIM + 1] + jnp.sum(s, axis=-1, keepdims=True)
        hv = num / jnp.maximum(jnp.abs(den), jnp.exp(-m_t))
        b_last = b_col[chunk - 1:chunk, :]
        g_col = b_last - b_col + i_col
        m_new = jnp.maximum(b_last + m_prev, jnp.max(g_col, axis=0, keepdims=True))
        decay = jnp.exp(b_last + m_prev - m_new)
        wkb = (kh.astype(F32) * jnp.exp(g_col - m_new)).astype(BF16)
        vx = jnp.concatenate([vh, ones], axis=1)
        for rb in range(B_HEAD_DIM // LANES):
            rs = slice(rb * LANES, (rb + 1) * LANES)
            c_new = decay * c_sc[h, rs, :] + lax.dot_general(wkb[:, rs], vx, TN_DIMS,
                                                             preferred_element_type=F32)
            c_sc[h, rs, :] = c_new
            cb_sc[h, rs, :] = c_new.astype(BF16)
        m_sc[h] = jnp.broadcast_to(m_new, m_sc.shape[1:])
        o_ref[:, hs] = ((_rms(hv, hg_ref[:, hs]) + skx_ref[:, hs]) * zg_ref[:, hs]).astype(BF16)


def _mlstm_scan(q, k, v, gates, skx, zg, tri, hg, riders):
    s = q.shape[0]
    blk = lambda: pl.BlockSpec((SCAN_CHUNK, B_INNER), lambda c: (c, 0))
    r_in, r_out, r_shape = _rider_specs(riders, s // SCAN_CHUNK, lambda c: c)
    outs = pl.pallas_call(
        functools.partial(_mlstm_kernel, n_riders=len(riders)),
        grid=(s // SCAN_CHUNK,),
        in_specs=[blk(), blk(), blk(), pl.BlockSpec((SCAN_CHUNK, LANES), lambda c: (c, 0)),
                  blk(), blk(), _resident(tri.shape), _resident(hg.shape), *r_in],
        out_specs=[blk(), *r_out],
        out_shape=[jax.ShapeDtypeStruct((s, B_INNER), BF16), *r_shape],
        scratch_shapes=[pltpu.VMEM((B_HEADS, B_HEAD_DIM, N_COLS), F32),
                        pltpu.VMEM((B_HEADS, B_HEAD_DIM, N_COLS), BF16),
                        pltpu.VMEM((B_HEADS, SUBLANES, LANES), F32)],
        compiler_params=_params(1),
        name="mlstm_scan",
    )(q, k, v, gates, skx, zg, tri, hg, *[w for w, _ in riders])
    return outs[0], outs[1:]


def _block_diag_tiles(w):
    nblk, blk, _ = w.shape
    rows = w.reshape(nblk * blk // MXU_TILE, MXU_TILE, blk)
    idx = jnp.arange(MXU_TILE) // blk
    same_block = (idx[:, None] == idx[None, :]).astype(w.dtype)
    return (jnp.tile(rows, (1, 1, MXU_TILE // blk)) * same_block[None]).astype(BF16)


def _row(v):
    return v.reshape(1, -1).astype(F32)


def kernel(x, p, a_norm, a_w_qkv, a_q_gain, a_k_gain, a_w_o, b_norm, b_w_up, b_conv_w, b_conv_b,
           b_w_q, b_w_k, b_w_v, b_w_gate, b_b_gate, b_h_gain, b_skip, b_w_down, mlp_norm, mlp_w1,
           mlp_w2, ple_norm, ple_w_gate, ple_w_proj):
    bsz, seq, _ = x.shape
    assert bsz == 1
    xs = x.reshape(seq, D_MODEL)
    ps = p.reshape(p.shape[0], seq, PLE_DIM)

    qg = jnp.tile(a_q_gain[0], (1, A_HEADS)).reshape(A_N_GROUPS, 1, D_MODEL) * A_HEAD_DIM ** -0.5
    kg = jnp.tile(a_k_gain[0], (1, A_HEADS)).reshape(A_N_GROUPS, 1, D_MODEL)
    seg = np.arange(MXU_TILE) // A_HEAD_DIM
    ones = (seg[:, None] == seg[None, :]).astype(BF16)
    low = np.broadcast_to(np.arange(LANES) < A_HEAD_DIM, (Q_BLOCK, LANES))
    hmask = np.stack([low, ~low]).astype(BF16)
    expand = (np.arange(LANES)[:, None] == np.arange(D_MODEL)[None, :] // A_HEAD_DIM).astype(BF16)
    qkv_groups = _qkv_proj(xs, _row(a_norm[0]), a_w_qkv[0].astype(BF16), qg, kg, ones)
    riders = ([(mlp_w1, 0), (a_w_o, 0)],
              [(mlp_w2, 0), (ple_w_gate, 0), (ple_w_proj, 0)],
              [(b_w_up, 0), (ple_w_proj, 1)])
    stats, cast = zip(*[_attention_group(qkv_groups[grp], grp, dil, _attn_bias(dil), hmask,
                                         riders[grp]) for grp, (_, dil) in enumerate(A_GROUPS)])
    sums, maxs, dens = zip(*stats)
    (w1_0, wo_0), (w2_0, wg_0, wp_0), (wup, wp_1) = cast
    xs = _post_mixer(xs, (sums, maxs, dens, expand), wo_0, _row(mlp_norm[0]), w1_0, w2_0,
                     _row(ple_norm[0]), wg_0, ps, 0, wp_0)

    wg = jnp.zeros((3 * B_INNER, LANES), F32).at[:, :2 * B_HEADS].set(b_w_gate[0]).astype(BF16)
    bg = jnp.zeros((1, LANES), F32).at[0, :2 * B_HEADS].set(b_b_gate[0])
    q, k, v, skx, zg, gates = _up_conv(xs, _row(b_norm[0]), wup, b_conv_w[0],
                                       _row(b_conv_b[0]), _row(b_skip[0]),
                                       _block_diag_tiles(b_w_q[0]), _block_diag_tiles(b_w_k[0]),
                                       _block_diag_tiles(b_w_v[0]), wg, bg)
    tri = (np.arange(SCAN_CHUNK)[:, None] >= np.arange(SCAN_CHUNK)[None, :]).astype(np.float32)
    mix, (wo_1, w1_1, w2_1, wg_1) = _mlstm_scan(
        q, k, v, gates, skx, zg, tri, _row(b_h_gain[0]),
        [(b_w_down, 0), (mlp_w1, 1), (mlp_w2, 1), (ple_w_gate, 1)])
    xs = _post_mixer(xs, mix, wo_1, _row(mlp_norm[1]), w1_1, w2_1, _row(ple_norm[1]), wg_1, ps, 1,
                     wp_1)
    return xs.reshape(bsz, seq, D_MODEL)
```

```python
import functools

import jax
import jax.numpy as jnp
import numpy as np
from jax import lax
from jax.experimental import pallas as pl
from jax.experimental.pallas import tpu as pltpu

F32 = jnp.float32
BF16 = jnp.bfloat16

D_MODEL = 1024
A_HEADS = 16
A_HEAD_DIM = D_MODEL // A_HEADS
A_GROUPS = ((128, 1), (512, 4), (2048, 16))
A_N_GROUPS = len(A_GROUPS)
Q_BLOCK = 128
TILES_PER_STEP = 8
MAX_STORE_STRIDE = 4
B_INNER = 2 * D_MODEL
B_HEADS = 4
B_HEAD_DIM = B_INNER // B_HEADS
B_CONV = 4
SCAN_CHUNK = 256
D_FF = 4 * D_MODEL
PLE_DIM = 256
EPS = 1e-6
NEG_INF = -1e30

MXU_TILE = 256
LANES = 128
SUBLANES = 8
BF16_ROWS = 16
N_SLABS = D_MODEL // LANES
VMEM_LIMIT_BYTES = 60000 * 1024
ROW_TILE = 512

NT_DIMS = (((1,), (1,)), ((), ()))
TN_DIMS = (((0,), (0,)), ((), ()))


def _params(n_axes):
    return pltpu.CompilerParams(dimension_semantics=("arbitrary",) * n_axes,
                                vmem_limit_bytes=VMEM_LIMIT_BYTES)


def _resident(shape):
    return pl.BlockSpec(shape, lambda *_: (0,) * len(shape), pipeline_mode=pl.Buffered(1))


def _rms(x, g):
    ms = jnp.mean(x * x, axis=-1, keepdims=True)
    return x * lax.rsqrt(ms + EPS) * g


def _rider_specs(jobs, n_steps, step_of):
    in_specs, out_specs, out_shape = [], [], []
    for w, layer in jobs:
        _, r, c = w.shape
        rows = r // n_steps
        assert rows * n_steps == r and rows % BF16_ROWS == 0, (w.shape, n_steps)
        in_specs.append(pl.BlockSpec((None, rows, c),
                                     lambda *g, layer=layer: (layer, step_of(*g), 0)))
        out_specs.append(pl.BlockSpec((rows, c), lambda *g: (step_of(*g), 0)))
        out_shape.append(jax.ShapeDtypeStruct((r, c), BF16))
    return in_specs, out_specs, out_shape


def _run_riders(src_refs, dst_refs):
    for src, dst in zip(src_refs, dst_refs):
        dst[...] = src[...].astype(BF16)


def _qkv_kernel(x_ref, g_ref, w_ref, qg_ref, kg_ref, ones_ref, *rest):
    out_refs, hn_sc = rest[:A_N_GROUPS], rest[A_N_GROUPS]
    hn = _rms(x_ref[...], g_ref[...])
    tm = hn.shape[0]
    for j in range(N_SLABS):
        hn_sc[j] = hn[:, j * LANES:(j + 1) * LANES]
    for grp, (_, dil) in enumerate(A_GROUPS):
        o_ref = out_refs[grp]
        n = tm // dil
        if dil == 1:
            hg = hn.astype(BF16)
        else:
            hg = jnp.concatenate(
                [jnp.concatenate([hn_sc[j, pl.ds(r, n, stride=dil), :] for j in range(N_SLABS)],
                                 axis=1) for r in range(dil)], axis=0).astype(BF16)
        for kind in range(3):
            c = 3 * grp + kind
            y = jnp.dot(hg, w_ref[:, c * D_MODEL:(c + 1) * D_MODEL], preferred_element_type=F32)
            if kind < 2:
                y2 = (y * y).astype(BF16)
                parts = [jnp.dot(y2[:, t * MXU_TILE:(t + 1) * MXU_TILE], ones_ref[...],
                                 preferred_element_type=F32) for t in range(D_MODEL // MXU_TILE)]
                ss = jnp.concatenate(parts, axis=1)
                gain = qg_ref[grp] if kind == 0 else kg_ref[grp]
                y = y * lax.rsqrt(ss * (1.0 / A_HEAD_DIM) + EPS) * gain
            yb = y.astype(BF16)
            ks = slice(kind * D_MODEL, (kind + 1) * D_MODEL)
            for r in range(dil):
                o_ref[r, :, ks] = yb[r * n:(r + 1) * n]


def _qkv_proj(x, g, w, qg, kg, ones):
    s = x.shape[0]
    out_specs, out_shape = [], []
    for _, dil in A_GROUPS:
        out_specs.append(pl.BlockSpec((dil, ROW_TILE // dil, 3 * D_MODEL), lambda i: (0, i, 0)))
        out_shape.append(jax.ShapeDtypeStruct((dil, s // dil, 3 * D_MODEL), BF16))
    return pl.pallas_call(
        _qkv_kernel,
        grid=(s // ROW_TILE,),
        in_specs=[pl.BlockSpec((ROW_TILE, D_MODEL), lambda i: (i, 0)),
                  _resident(g.shape), _resident(w.shape), _resident(qg.shape),
                  _resident(kg.shape), _resident(ones.shape)],
        out_specs=out_specs,
        out_shape=out_shape,
        scratch_shapes=[pltpu.VMEM((N_SLABS, ROW_TILE, LANES), F32)],
        compiler_params=_params(1),
        name="qkv_proj",
    )(x, g, w, qg, kg, ones)


def _attn_kernel(q_ref, kp_ref, kc_ref, vp_ref, vc_ref, bias_ref, hmask_ref, *rest, dil, n_riders):
    o_ref, m_ref, l_ref = rest[n_riders:n_riders + 3]
    _run_riders(rest[:n_riders], rest[n_riders + 3:])
    n_res = q_ref.shape[0]
    n_tiles = q_ref.shape[1] // Q_BLOCK
    planes = m_ref.shape[0]
    stride = dil // planes
    lane = lax.broadcasted_iota(jnp.int32, (Q_BLOCK, LANES), 1)
    lo = lane < A_HEAD_DIM
    ones = jnp.ones((2 * Q_BLOCK, LANES), BF16)
    for sub in range(n_res):
        plane = sub % planes
        pos = pl.program_id(1) * (n_res // planes) + sub // planes
        for tile in range(n_tiles):
            cur = slice(tile * Q_BLOCK, (tile + 1) * Q_BLOCK)
            both = slice((tile - 1) * Q_BLOCK, (tile + 1) * Q_BLOCK)
            if stride > 1:
                rows = pl.ds(tile * Q_BLOCK * stride + pos, Q_BLOCK, stride=stride)
            else:
                rows = cur
            if tile == 0:
                sel = (pl.program_id(0) == 0).astype(jnp.int32)
            else:
                sel = 0
            m_tile = jnp.zeros((Q_BLOCK, LANES), F32)
            l_tile = jnp.ones((Q_BLOCK, LANES), F32)
            for pair in range(A_HEADS // 2):
                cs = slice(pair * LANES, (pair + 1) * LANES)
                q2 = q_ref[sub, cur, cs]
                if tile == 0:
                    kcat = jnp.concatenate([kp_ref[sub, :, cs], kc_ref[sub, cur, cs]], axis=0)
                    vcat = jnp.concatenate([vp_ref[sub, :, cs], vc_ref[sub, cur, cs]], axis=0)
                else:
                    kcat = kc_ref[sub, both, cs]
                    vcat = vc_ref[sub, both, cs]
                qq = jnp.concatenate([q2 * hmask_ref[0], q2 * hmask_ref[1]], axis=0)
                s = lax.dot_general(qq, kcat, NT_DIMS, preferred_element_type=F32)
                s = s + bias_ref[sel, pair]
                m = jnp.max(s, axis=-1, keepdims=True)
                p = jnp.exp((s - m).astype(BF16))
                ul = jnp.dot(p, jnp.concatenate([vcat, ones], axis=1), preferred_element_type=F32)
                u, l = ul[:, :LANES], ul[:, LANES:]
                for hh in range(2):
                    hr = slice(hh * Q_BLOCK, (hh + 1) * Q_BLOCK)
                    m_tile = jnp.where(lane == 2 * pair + hh, m[hr], m_tile)
                    l_tile = jnp.where(lane == 2 * pair + hh, l[hr], l_tile)
                o_ref[pair, plane, rows, :] = jnp.where(lo, u[:Q_BLOCK], u[Q_BLOCK:])
            m_ref[plane, rows, :] = m_tile
            l_ref[plane, rows, :] = l_tile


def _attn_bias(dil):
    slopes = np.asarray([2.0 ** (-8.0 * (h + 1) / A_HEADS) for h in range(A_HEADS)], np.float32)
    row = np.arange(Q_BLOCK)[:, None]
    col = np.arange(Q_BLOCK)[None, :]

    def table(steps, valid):
        dist = (steps * dil).astype(np.float32)
        return np.where(valid[None], -(slopes[:, None, None] * dist[None]), np.float32(NEG_INF))

    prev = table(Q_BLOCK + row - col, col >= row)
    cur = table(row - col, col <= row)
    normal = np.concatenate([prev, cur], axis=-1)
    firstb = np.concatenate([np.full_like(prev, NEG_INF), cur], axis=-1)
    tables = np.stack([normal, firstb]).astype(np.float32)
    return tables.reshape(2, A_HEADS // 2, 2 * Q_BLOCK, 2 * Q_BLOCK)


def _attention_group(qkv_g, grp, dil, bias, hmask, riders):
    _, sd, _ = qkv_g.shape
    s = sd * dil
    n_res = min(TILES_PER_STEP, dil)
    n_tiles = TILES_PER_STEP // n_res
    blk = n_tiles * Q_BLOCK
    planes = max(1, dil // MAX_STORE_STRIDE)
    assert n_res % planes == 0
    prow = blk * dil // planes

    def col_spec(off, prev):
        if prev:
            return pl.BlockSpec((n_res, Q_BLOCK, D_MODEL),
                                lambda b, r: (r, jnp.maximum(b * n_tiles - 1, 0), off))
        return pl.BlockSpec((n_res, blk, D_MODEL), lambda b, r: (r, b, off))

    grid = (sd // blk, dil // n_res)
    r_in, r_out, r_shape = _rider_specs(riders, grid[0] * grid[1], lambda b, r: b * grid[1] + r)
    outs = pl.pallas_call(
        functools.partial(_attn_kernel, dil=dil, n_riders=len(riders)),
        grid=grid,
        in_specs=[col_spec(0, False), col_spec(1, True), col_spec(1, False),
                  col_spec(2, True), col_spec(2, False),
                  _resident(bias.shape), _resident(hmask.shape), *r_in],
        out_specs=[pl.BlockSpec((N_SLABS, planes, prow, LANES), lambda b, r: (0, 0, b, 0)),
                   pl.BlockSpec((planes, prow, LANES), lambda b, r: (0, b, 0)),
                   pl.BlockSpec((planes, prow, LANES), lambda b, r: (0, b, 0)), *r_out],
        out_shape=[jax.ShapeDtypeStruct((N_SLABS, planes, s // planes, LANES), F32),
                   jax.ShapeDtypeStruct((planes, s // planes, LANES), F32),
                   jax.ShapeDtypeStruct((planes, s // planes, LANES), F32), *r_shape],
        compiler_params=_params(2),
        name=f"dilated_attn_g{grp}",
    )(qkv_g, qkv_g, qkv_g, qkv_g, qkv_g, bias, hmask, *[w for w, _ in riders])
    return outs[:3], outs[3:]


def _natural_rows(ref, lead, stage_ref):
    planes = ref.shape[-3]
    if planes == 1:
        return ref[(*lead, 0)]
    for b in range(planes):
        stage_ref[pl.ds(b, ref.shape[-2], stride=planes), :] = ref[(*lead, b)]
    return stage_ref[...]


def _merge_groups(u_refs, m_refs, l_refs, expand_ref, stage_ref):
    ms = [_natural_rows(r, (), stage_ref.at[N_SLABS]) for r in m_refs]
    ls = [_natural_rows(r, (), stage_ref.at[N_SLABS + 1]) for r in l_refs]
    top = functools.reduce(jnp.maximum, ms)
    es = [jnp.exp(v - top) for v in ms]
    den = functools.reduce(jnp.add, [e * l for e, l in zip(es, ls)])
    merged = None
    for u_ref, e in zip(u_refs, es):
        wide = jnp.dot((e / den).astype(BF16), expand_ref[...], preferred_element_type=F32)
        u = jnp.concatenate([_natural_rows(u_ref, (j,), stage_ref.at[j]) for j in range(N_SLABS)],
                            axis=1)
        merged = wide * u if merged is None else merged + wide * u
    return merged.astype(BF16)


def _post_kernel(*refs, merge):
    x_ref = refs[0]
    if merge:
        g = A_N_GROUPS
        mix = _merge_groups(refs[1:1 + g], refs[1 + g:1 + 2 * g], refs[1 + 2 * g:1 + 3 * g],
                            refs[1 + 3 * g], refs[-1])
        refs = refs[2 + 3 * g:-1]
    else:
        mix = refs[1][...]
        refs = refs[2:]
    wo_ref, gm_ref, w1_ref, w2_ref, gp_ref, wg_ref, p_ref, wp_ref, o_ref = refs
    x = x_ref[...] + jnp.dot(mix, wo_ref[...], preferred_element_type=F32)
    hn = _rms(x, gm_ref[...]).astype(BF16)
    acc = jnp.zeros_like(x)
    for c in range(D_FF // D_MODEL):
        cs = slice(c * D_MODEL, (c + 1) * D_MODEL)
        a = jnp.dot(hn, w1_ref[:, cs], preferred_element_type=F32)
        a = jnp.square(jnp.maximum(a, 0.0)).astype(BF16)
        acc = acc + jnp.dot(a, w2_ref[cs, :], preferred_element_type=F32)
    x = x + acc
    gate = jax.nn.sigmoid(jnp.dot(_rms(x, gp_ref[...]).astype(BF16), wg_ref[...],
                                  preferred_element_type=F32))
    emb = jnp.dot(p_ref[...].astype(BF16), wp_ref[...], preferred_element_type=F32)
    o_ref[...] = x + gate * emb


def _post_mixer(x, mix, wo, gm, w1, w2, gp, wg, p, layer, wp):
    s = x.shape[0]
    row = lambda width: pl.BlockSpec((ROW_TILE, width), lambda i: (i, 0))
    merge = isinstance(mix, tuple)
    if merge:
        sums, maxs, dens, expand = mix
        mix_args = [*sums, *maxs, *dens, expand]
        mix_specs = [pl.BlockSpec((N_SLABS, u.shape[1], ROW_TILE // u.shape[1], LANES),
                                  lambda i: (0, 0, i, 0)) for u in sums]
        mix_specs += [pl.BlockSpec((v.shape[0], ROW_TILE // v.shape[0], LANES), lambda i: (0, i, 0))
                      for v in (*maxs, *dens)]
        mix_specs.append(_resident(expand.shape))
        scratch = [pltpu.VMEM((N_SLABS + 2, ROW_TILE, LANES), F32)]
    else:
        mix_args = [mix]
        mix_specs = [row(mix.shape[1])]
        scratch = []
    return pl.pallas_call(
        functools.partial(_post_kernel, merge=merge),
        grid=(s // ROW_TILE,),
        in_specs=[row(D_MODEL), *mix_specs, _resident(wo.shape), _resident(gm.shape),
                  _resident(w1.shape), _resident(w2.shape), _resident(gp.shape),
                  _resident(wg.shape),
                  pl.BlockSpec((None, ROW_TILE, PLE_DIM), lambda i: (layer, i, 0)),
                  _resident(wp.shape)],
        out_specs=row(D_MODEL),
        out_shape=jax.ShapeDtypeStruct((s, D_MODEL), F32),
        scratch_shapes=scratch,
        compiler_params=_params(1),
        name="post_mixer",
    )(x, *mix_args, wo, gm, w1, w2, gp, wg, p, wp)


HALO = SUBLANES


def _up_conv_kernel(x_ref, g_ref, wup_ref, cw_ref, cb_ref, sk_ref, wq_ref, wk_ref, wv_ref, wg_ref,
                    bg_ref, q_ref, k_ref, v_ref, skx_ref, zg_ref, gt_ref, halo_sc, ext_sc):
    @pl.when(pl.program_id(0) == 0)
    def _():
        halo_sc[...] = jnp.zeros_like(halo_sc)

    hn = _rms(x_ref[...], g_ref[...]).astype(BF16)
    tm = hn.shape[0]
    for c in range(B_INNER // D_MODEL):
        cs = slice(c * D_MODEL, (c + 1) * D_MODEL)
        z = jnp.dot(hn, wup_ref[:, B_INNER + c * D_MODEL:B_INNER + (c + 1) * D_MODEL],
                    preferred_element_type=F32)
        zg_ref[:, cs] = jax.nn.silu(z)
    gates = jnp.broadcast_to(bg_ref[...], gt_ref.shape)
    for t in range(B_INNER // MXU_TILE):
        ts = slice(t * MXU_TILE, (t + 1) * MXU_TILE)
        xm = jnp.dot(hn, wup_ref[:, ts], preferred_element_type=F32)
        ext_sc[:HALO, :] = halo_sc[:, ts]
        ext_sc[HALO:, :] = xm
        halo_sc[:, ts] = xm[tm - HALO:]
        y = cb_ref[:, ts] + cw_ref[B_CONV - 1:B_CONV, ts] * xm
        for j in range(1, B_CONV):
            y = y + cw_ref[B_CONV - 1 - j:B_CONV - j, ts] * ext_sc[pl.ds(HALO - j, tm), :]
        xc = jax.nn.silu(y)
        skx_ref[:, ts] = sk_ref[:, ts] * xc
        xcb = xc.astype(BF16)
        qf = jnp.dot(xcb, wq_ref[t], preferred_element_type=F32)
        kf = jnp.dot(xcb, wk_ref[t], preferred_element_type=F32)
        vf = jnp.dot(xm.astype(BF16), wv_ref[t], preferred_element_type=F32)
        qb, kb, vb = qf.astype(BF16), kf.astype(BF16), vf.astype(BF16)
        q_ref[:, ts] = qb
        k_ref[:, ts] = (kf * B_HEAD_DIM ** -0.5).astype(BF16)
        v_ref[:, ts] = vb
        for part, val in enumerate((qb, kb, vb)):
            ws = slice(part * B_INNER + t * MXU_TILE, part * B_INNER + (t + 1) * MXU_TILE)
            gates = gates + jnp.dot(val, wg_ref[ws, :], preferred_element_type=F32)
    gt_ref[...] = gates


def _up_conv(x, g, wup, cw, cb, sk, wq, wk, wv, wg, bg):
    s = x.shape[0]
    row = lambda: pl.BlockSpec((ROW_TILE, B_INNER), lambda i: (i, 0))
    return pl.pallas_call(
        _up_conv_kernel,
        grid=(s // ROW_TILE,),
        in_specs=[pl.BlockSpec((ROW_TILE, D_MODEL), lambda i: (i, 0)),
                  _resident(g.shape), _resident(wup.shape),
                  _resident(cw.shape), _resident(cb.shape), _resident(sk.shape),
                  _resident(wq.shape), _resident(wk.shape), _resident(wv.shape),
                  _resident(wg.shape), _resident(bg.shape)],
        out_specs=[row(), row(), row(), row(), row(),
                   pl.BlockSpec((ROW_TILE, LANES), lambda i: (i, 0))],
        out_shape=[jax.ShapeDtypeStruct((s, B_INNER), BF16)] * 3
        + [jax.ShapeDtypeStruct((s, B_INNER), F32)] * 2 + [jax.ShapeDtypeStruct((s, LANES), F32)],
        scratch_shapes=[pltpu.VMEM((HALO, B_INNER), F32),
                        pltpu.VMEM((HALO + ROW_TILE, MXU_TILE), F32)],
        compiler_params=_params(1),
        name="up_conv_qkv_gates",
    )(x, g, wup, cw, cb, sk, wq, wk, wv, wg, bg)


N_COLS = B_HEAD_DIM + LANES


def _mlstm_kernel(q_ref, k_ref, v_ref, g_ref, skx_ref, zg_ref, tri_ref, hg_ref, *rest, n_riders):
    o_ref = rest[n_riders]
    c_sc, cb_sc, m_sc = rest[2 * n_riders + 1:]
    _run_riders(rest[:n_riders], rest[n_riders + 1:2 * n_riders + 1])

    @pl.when(pl.program_id(0) == 0)
    def _():
        c_sc[...] = jnp.zeros_like(c_sc)
        cb_sc[...] = jnp.zeros_like(cb_sc)
        m_sc[...] = jnp.full_like(m_sc, NEG_INF)

    chunk = q_ref.shape[0]
    g_all = g_ref[...]
    lf_all = jax.nn.log_sigmoid(g_all)
    b_all = jnp.dot(tri_ref[...], lf_all, preferred_element_type=F32,
                    precision=lax.Precision.HIGHEST)
    b_t = b_all.T
    g_t = g_all.T
    row = lax.broadcasted_iota(jnp.int32, (chunk, chunk), 0)
    col = lax.broadcasted_iota(jnp.int32, (chunk, chunk), 1)
    causal = col <= row
    ones = jnp.ones((chunk, LANES), BF16)
    for h in range(B_HEADS):
        hs = slice(h * B_HEAD_DIM, (h + 1) * B_HEAD_DIM)
        i_col = g_all[:, h:h + 1]
        b_col = b_all[:, B_HEADS + h:B_HEADS + h + 1]
        i_row = g_t[h:h + 1, :]
        b_row = b_t[B_HEADS + h:B_HEADS + h + 1, :]
        m_prev = m_sc[h, 0:1, 0:1]
        dmat = jnp.where(causal, b_col - b_row + i_row, NEG_INF)
        inter = b_col + m_prev
        m_t = jnp.maximum(inter, jnp.max(dmat, axis=-1, keepdims=True))
        qh = q_ref[:, hs]
        kh = k_ref[:, hs]
        vh = v_ref[:, hs]
        s = lax.dot_general(qh, kh, NT_DIMS, preferred_element_type=F32) * jnp.exp(dmat - m_t)
        sc = jnp.exp(inter - m_t)
        qc = jnp.dot(qh, cb_sc[h], preferred_element_type=F32)
        num = sc * qc[:, :B_HEAD_DIM] + jnp.dot(s.astype(BF16), vh, preferred_element_type=F32)
        den = sc * qc[:, B_HEAD_DIM:B_HEAD_DIM + 1] + jnp.sum(s, axis=-1, keepdims=True)
        hv = num / jnp.maximum(jnp.abs(den), jnp.exp(-m_t))
        b_last = b_col[chunk - 1:chunk, :]
        g_col = b_last - b_col + i_col
        m_new = jnp.maximum(b_last + m_prev, jnp.max(g_col, axis=0, keepdims=True))
        decay = jnp.exp(b_last + m_prev - m_new)
        wkb = (kh.astype(F32) * jnp.exp(g_col - m_new)).astype(BF16)
        vx = jnp.concatenate([vh, ones], axis=1)
        for rb in range(B_HEAD_DIM // LANES):
            rs = slice(rb * LANES, (rb + 1) * LANES)
            c_new = decay * c_sc[h, rs, :] + lax.dot_general(wkb[:, rs], vx, TN_DIMS,
                                                             preferred_element_type=F32)
            c_sc[h, rs, :] = c_new
            cb_sc[h, rs, :] = c_new.astype(BF16)
        m_sc[h] = jnp.broadcast_to(m_new, m_sc.shape[1:])
        o_ref[:, hs] = ((_rms(hv, hg_ref[:, hs]) + skx_ref[:, hs]) * zg_ref[:, hs]).astype(BF16)


def _mlstm_scan(q, k, v, gates, skx, zg, tri, hg, riders):
    s = q.shape[0]
    blk = lambda: pl.BlockSpec((SCAN_CHUNK, B_INNER), lambda c: (c, 0))
    r_in, r_out, r_shape = _rider_specs(riders, s // SCAN_CHUNK, lambda c: c)
    outs = pl.pallas_call(
        functools.partial(_mlstm_kernel, n_riders=len(riders)),
        grid=(s // SCAN_CHUNK,),
        in_specs=[blk(), blk(), blk(), pl.BlockSpec((SCAN_CHUNK, LANES), lambda c: (c, 0)),
                  blk(), blk(), _resident(tri.shape), _resident(hg.shape), *r_in],
        out_specs=[blk(), *r_out],
        out_shape=[jax.ShapeDtypeStruct((s, B_INNER), BF16), *r_shape],
        scratch_shapes=[pltpu.VMEM((B_HEADS, B_HEAD_DIM, N_COLS), F32),
                        pltpu.VMEM((B_HEADS, B_HEAD_DIM, N_COLS), BF16),
                        pltpu.VMEM((B_HEADS, SUBLANES, LANES), F32)],
        compiler_params=_params(1),
        name="mlstm_scan",
    )(q, k, v, gates, skx, zg, tri, hg, *[w for w, _ in riders])
    return outs[0], outs[1:]


def _block_diag_tiles(w):
    nblk, blk, _ = w.shape
    rows = w.reshape(nblk * blk // MXU_TILE, MXU_TILE, blk)
    idx = jnp.arange(MXU_TILE) // blk
    same_block = (idx[:, None] == idx[None, :]).astype(w.dtype)
    return (jnp.tile(rows, (1, 1, MXU_TILE // blk)) * same_block[None]).astype(BF16)


def _row(v):
    return v.reshape(1, -1).astype(F32)


def kernel(x, p, a_norm, a_w_qkv, a_q_gain, a_k_gain, a_w_o, b_norm, b_w_up, b_conv_w, b_conv_b,
           b_w_q, b_w_k, b_w_v, b_w_gate, b_b_gate, b_h_gain, b_skip, b_w_down, mlp_norm, mlp_w1,
           mlp_w2, ple_norm, ple_w_gate, ple_w_proj):
    bsz, seq, _ = x.shape
    assert bsz == 1
    xs = x.reshape(seq, D_MODEL)
    ps = p.reshape(p.shape[0], seq, PLE_DIM)

    qg = jnp.tile(a_q_gain[0], (1, A_HEADS)).reshape(A_N_GROUPS, 1, D_MODEL) * A_HEAD_DIM ** -0.5
    kg = jnp.tile(a_k_gain[0], (1, A_HEADS)).reshape(A_N_GROUPS, 1, D_MODEL)
    seg = np.arange(MXU_TILE) // A_HEAD_DIM
    ones = (seg[:, None] == seg[None, :]).astype(BF16)
    low = np.broadcast_to(np.arange(LANES) < A_HEAD_DIM, (Q_BLOCK, LANES))
    hmask = np.stack([low, ~low]).astype(BF16)
    expand = (np.arange(LANES)[:, None] == np.arange(D_MODEL)[None, :] // A_HEAD_DIM).astype(BF16)
    qkv_groups = _qkv_proj(xs, _row(a_norm[0]), a_w_qkv[0].astype(BF16), qg, kg, ones)
    riders = ([(mlp_w1, 0), (a_w_o, 0)],
              [(mlp_w2, 0), (ple_w_gate, 0), (ple_w_proj, 0)],
              [(b_w_up, 0), (ple_w_proj, 1)])
    stats, cast = zip(*[_attention_group(qkv_groups[grp], grp, dil, _attn_bias(dil), hmask,
                                         riders[grp]) for grp, (_, dil) in enumerate(A_GROUPS)])
    sums, maxs, dens = zip(*stats)
    (w1_0, wo_0), (w2_0, wg_0, wp_0), (wup, wp_1) = cast
    xs = _post_mixer(xs, (sums, maxs, dens, expand), wo_0, _row(mlp_norm[0]), w1_0, w2_0,
                     _row(ple_norm[0]), wg_0, ps, 0, wp_0)

    wg = jnp.zeros((3 * B_INNER, LANES), F32).at[:, :2 * B_HEADS].set(b_w_gate[0]).astype(BF16)
    bg = jnp.zeros((1, LANES), F32).at[0, :2 * B_HEADS].set(b_b_gate[0])
    q, k, v, skx, zg, gates = _up_conv(xs, _row(b_norm[0]), wup, b_conv_w[0],
                                       _row(b_conv_b[0]), _row(b_skip[0]),
                                       _block_diag_tiles(b_w_q[0]), _block_diag_tiles(b_w_k[0]),
                                       _block_diag_tiles(b_w_v[0]), wg, bg)
    tri = (np.arange(SCAN_CHUNK)[:, None] >= np.arange(SCAN_CHUNK)[None, :]).astype(np.float32)
    mix, (wo_1, w1_1, w2_1, wg_1) = _mlstm_scan(
        q, k, v, gates, skx, zg, tri, _row(b_h_gain[0]),
        [(b_w_down, 0), (mlp_w1, 1), (mlp_w2, 1), (ple_w_gate, 1)])
    xs = _post_mixer(xs, mix, wo_1, _row(mlp_norm[1]), w1_1, w2_1, _row(ple_norm[1]), wg_1, ps, 1,
                     wp_1)
    return xs.reshape(bsz, seq, D_MODEL)
```

```python
import functools

import jax
import jax.numpy as jnp
import numpy as np
from jax import lax
from jax.experimental import pallas as pl
from jax.experimental.pallas import tpu as pltpu

F32 = jnp.float32
BF16 = jnp.bfloat16

D_MODEL = 1024
A_HEADS = 16
A_HEAD_DIM = D_MODEL // A_HEADS
A_GROUPS = ((128, 1), (512, 4), (2048, 16))
A_N_GROUPS = len(A_GROUPS)
Q_BLOCK = 128
TILES_PER_STEP = 8
MAX_STORE_STRIDE = 4
B_INNER = 2 * D_MODEL
B_HEADS = 4
B_HEAD_DIM = B_INNER // B_HEADS
B_CONV = 4
SCAN_CHUNK = 256
D_FF = 4 * D_MODEL
PLE_DIM = 256
EPS = 1e-6
NEG_INF = -1e30

MXU_TILE = 256
LANES = 128
SUBLANES = 8
BF16_ROWS = 16
N_SLABS = D_MODEL // LANES
VMEM_LIMIT_BYTES = 60000 * 1024
ROW_TILE = 512

NT_DIMS = (((1,), (1,)), ((), ()))
TN_DIMS = (((0,), (0,)), ((), ()))


def _params(n_axes):
    return pltpu.CompilerParams(dimension_semantics=("arbitrary",) * n_axes,
                                vmem_limit_bytes=VMEM_LIMIT_BYTES)


def _resident(shape):
    return pl.BlockSpec(shape, lambda *_: (0,) * len(shape), pipeline_mode=pl.Buffered(1))


def _rms(x, g):
    ms = jnp.mean(x * x, axis=-1, keepdims=True)
    return x * lax.rsqrt(ms + EPS) * g


def _rider_specs(jobs, n_steps, step_of):
    in_specs, out_specs, out_shape = [], [], []
    for w, layer in jobs:
        _, r, c = w.shape
        rows = r // n_steps
        assert rows * n_steps == r and rows % BF16_ROWS == 0, (w.shape, n_steps)
        in_specs.append(pl.BlockSpec((None, rows, c),
                                     lambda *g, layer=layer: (layer, step_of(*g), 0)))
        out_specs.append(pl.BlockSpec((rows, c), lambda *g: (step_of(*g), 0)))
        out_shape.append(jax.ShapeDtypeStruct((r, c), BF16))
    return in_specs, out_specs, out_shape


def _run_riders(src_refs, dst_refs):
    for src, dst in zip(src_refs, dst_refs):
        dst[...] = src[...].astype(BF16)


def _qkv_kernel(x_ref, g_ref, w_ref, qg_ref, kg_ref, ones_ref, *rest):
    out_refs, hn_sc = rest[:A_N_GROUPS], rest[A_N_GROUPS]
    hn = _rms(x_ref[...], g_ref[...])
    tm = hn.shape[0]
    for j in range(N_SLABS):
        hn_sc[j] = hn[:, j * LANES:(j + 1) * LANES]
    for grp, (_, dil) in enumerate(A_GROUPS):
        o_ref = out_refs[grp]
        n = tm // dil
        if dil == 1:
            hg = hn.astype(BF16)
        else:
            hg = jnp.concatenate(
                [jnp.concatenate([hn_sc[j, pl.ds(r, n, stride=dil), :] for j in range(N_SLABS)],
                                 axis=1) for r in range(dil)], axis=0).astype(BF16)
        for kind in range(3):
            c = 3 * grp + kind
            y = jnp.dot(hg, w_ref[:, c * D_MODEL:(c + 1) * D_MODEL], preferred_element_type=F32)
            if kind < 2:
                y2 = (y * y).astype(BF16)
                parts = [jnp.dot(y2[:, t * MXU_TILE:(t + 1) * MXU_TILE], ones_ref[...],
                                 preferred_element_type=F32) for t in range(D_MODEL // MXU_TILE)]
                ss = jnp.concatenate(parts, axis=1)
                gain = qg_ref[grp] if kind == 0 else kg_ref[grp]
                y = y * lax.rsqrt(ss * (1.0 / A_HEAD_DIM) + EPS) * gain
            yb = y.astype(BF16)
            ks = slice(kind * D_MODEL, (kind + 1) * D_MODEL)
            for r in range(dil):
                o_ref[r, :, ks] = yb[r * n:(r + 1) * n]


def _qkv_proj(x, g, w, qg, kg, ones):
    s = x.shape[0]
    out_specs, out_shape = [], []
    for _, dil in A_GROUPS:
        out_specs.append(pl.BlockSpec((dil, ROW_TILE // dil, 3 * D_MODEL), lambda i: (0, i, 0)))
        out_shape.append(jax.ShapeDtypeStruct((dil, s // dil, 3 * D_MODEL), BF16))
    return pl.pallas_call(
        _qkv_kernel,
        grid=(s // ROW_TILE,),
        in_specs=[pl.BlockSpec((ROW_TILE, D_MODEL), lambda i: (i, 0)),
                  _resident(g.shape), _resident(w.shape), _resident(qg.shape),
                  _resident(kg.shape), _resident(ones.shape)],
        out_specs=out_specs,
        out_shape=out_shape,
        scratch_shapes=[pltpu.VMEM((N_SLABS, ROW_TILE, LANES), F32)],
        compiler_params=_params(1),
        name="qkv_proj",
    )(x, g, w, qg, kg, ones)


def _attn_kernel(q_ref, kp_ref, kc_ref, vp_ref, vc_ref, bias_ref, hmask_ref, *rest, dil, n_riders):
    o_ref, m_ref, l_ref = rest[n_riders:n_riders + 3]
    _run_riders(rest[:n_riders], rest[n_riders + 3:])
    n_res = q_ref.shape[0]
    n_tiles = q_ref.shape[1] // Q_BLOCK
    planes = m_ref.shape[0]
    stride = dil // planes
    lane = lax.broadcasted_iota(jnp.int32, (Q_BLOCK, LANES), 1)
    lo = lane < A_HEAD_DIM
    ones = jnp.ones((2 * Q_BLOCK, LANES), BF16)
    for sub in range(n_res):
        plane = sub % planes
        pos = pl.program_id(1) * (n_res // planes) + sub // planes
        for tile in range(n_tiles):
            cur = slice(tile * Q_BLOCK, (tile + 1) * Q_BLOCK)
            both = slice((tile - 1) * Q_BLOCK, (tile + 1) * Q_BLOCK)
            if stride > 1:
                rows = pl.ds(tile * Q_BLOCK * stride + pos, Q_BLOCK, stride=stride)
            else:
                rows = cur
            if tile == 0:
                sel = (pl.program_id(0) == 0).astype(jnp.int32)
            else:
                sel = 0
            m_tile = jnp.zeros((Q_BLOCK, LANES), F32)
            l_tile = jnp.ones((Q_BLOCK, LANES), F32)
            for pair in range(A_HEADS // 2):
                cs = slice(pair * LANES, (pair + 1) * LANES)
                q2 = q_ref[sub, cur, cs]
                if tile == 0:
                    kcat = jnp.concatenate([kp_ref[sub, :, cs], kc_ref[sub, cur, cs]], axis=0)
                    vcat = jnp.concatenate([vp_ref[sub, :, cs], vc_ref[sub, cur, cs]], axis=0)
                else:
                    kcat = kc_ref[sub, both, cs]
                    vcat = vc_ref[sub, both, cs]
                qq = jnp.concatenate([q2 * hmask_ref[0], q2 * hmask_ref[1]], axis=0)
                s = lax.dot_general(qq, kcat, NT_DIMS, preferred_element_type=F32)
                s = s + bias_ref[sel, pair]
                m = jnp.max(s, axis=-1, keepdims=True)
                p = jnp.exp((s - m).astype(BF16))
                ul = jnp.dot(p, jnp.concatenate([vcat, ones], axis=1), preferred_element_type=F32)
                u, l = ul[:, :LANES], ul[:, LANES:]
                for hh in range(2):
                    hr = slice(hh * Q_BLOCK, (hh + 1) * Q_BLOCK)
                    m_tile = jnp.where(lane == 2 * pair + hh, m[hr], m_tile)
                    l_tile = jnp.where(lane == 2 * pair + hh, l[hr], l_tile)
                o_ref[pair, plane, rows, :] = jnp.where(lo, u[:Q_BLOCK], u[Q_BLOCK:])
            m_ref[plane, rows, :] = m_tile
            l_ref[plane, rows, :] = l_tile


def _attn_bias(dil):
    slopes = np.asarray([2.0 ** (-8.0 * (h + 1) / A_HEADS) for h in range(A_HEADS)], np.float32)
    row = np.arange(Q_BLOCK)[:, None]
    col = np.arange(Q_BLOCK)[None, :]

    def table(steps, valid):
        dist = (steps * dil).astype(np.float32)
        return np.where(valid[None], -(slopes[:, None, None] * dist[None]), np.float32(NEG_INF))

    prev = table(Q_BLOCK + row - col, col >= row)
    cur = table(row - col, col <= row)
    normal = np.concatenate([prev, cur], axis=-1)
    firstb = np.concatenate([np.full_like(prev, NEG_INF), cur], axis=-1)
    tables = np.stack([normal, firstb]).astype(np.float32)
    return tables.reshape(2, A_HEADS // 2, 2 * Q_BLOCK, 2 * Q_BLOCK)


def _attention_group(qkv_g, grp, dil, bias, hmask, riders):
    _, sd, _ = qkv_g.shape
    s = sd * dil
    n_res = min(TILES_PER_STEP, dil)
    n_tiles = TILES_PER_STEP // n_res
    blk = n_tiles * Q_BLOCK
    planes = max(1, dil // MAX_STORE_STRIDE)
    assert n_res % planes == 0
    prow = blk * dil // planes

    def col_spec(off, prev):
        if prev:
            return pl.BlockSpec((n_res, Q_BLOCK, D_MODEL),
                                lambda b, r: (r, jnp.maximum(b * n_tiles - 1, 0), off))
        return pl.BlockSpec((n_res, blk, D_MODEL), lambda b, r: (r, b, off))

    grid = (sd // blk, dil // n_res)
    r_in, r_out, r_shape = _rider_specs(riders, grid[0] * grid[1], lambda b, r: b * grid[1] + r)
    outs = pl.pallas_call(
        functools.partial(_attn_kernel, dil=dil, n_riders=len(riders)),
        grid=grid,
        in_specs=[col_spec(0, False), col_spec(1, True), col_spec(1, False),
                  col_spec(2, True), col_spec(2, False),
                  _resident(bias.shape), _resident(hmask.shape), *r_in],
        out_specs=[pl.BlockSpec((N_SLABS, planes, prow, LANES), lambda b, r: (0, 0, b, 0)),
                   pl.BlockSpec((planes, prow, LANES), lambda b, r: (0, b, 0)),
                   pl.BlockSpec((planes, prow, LANES), lambda b, r: (0, b, 0)), *r_out],
        out_shape=[jax.ShapeDtypeStruct((N_SLABS, planes, s // planes, LANES), F32),
                   jax.ShapeDtypeStruct((planes, s // planes, LANES), F32),
                   jax.ShapeDtypeStruct((planes, s // planes, LANES), F32), *r_shape],
        compiler_params=_params(2),
        name=f"dilated_attn_g{grp}",
    )(qkv_g, qkv_g, qkv_g, qkv_g, qkv_g, bias, hmask, *[w for w, _ in riders])
    return outs[:3], outs[3:]


def _natural_rows(ref, lead, stage_ref):
    planes = ref.shape[-3]
    if planes == 1:
        return ref[(*lead, 0)]
    for b in range(planes):
        stage_ref[pl.ds(b, ref.shape[-2], stride=planes), :] = ref[(*lead, b)]
    return stage_ref[...]


def _merge_groups(u_refs, m_refs, l_refs, expand_ref, stage_ref):
    ms = [_natural_rows(r, (), stage_ref.at[N_SLABS]) for r in m_refs]
    ls = [_natural_rows(r, (), stage_ref.at[N_SLABS + 1]) for r in l_refs]
    top = functools.reduce(jnp.maximum, ms)
    es = [jnp.exp(v - top) for v in ms]
    den = functools.reduce(jnp.add, [e * l for e, l in zip(es, ls)])
    merged = None
    for u_ref, e in zip(u_refs, es):
        wide = jnp.dot((e / den).astype(BF16), expand_ref[...], preferred_element_type=F32)
        u = jnp.concatenate([_natural_rows(u_ref, (j,), stage_ref.at[j]) for j in range(N_SLABS)],
                            axis=1)
        merged = wide * u if merged is None else merged + wide * u
    return merged.astype(BF16)


def _post_kernel(*refs, merge):
    x_ref = refs[0]
    if merge:
        g = A_N_GROUPS
        mix = _merge_groups(refs[1:1 + g], refs[1 + g:1 + 2 * g], refs[1 + 2 * g:1 + 3 * g],
                            refs[1 + 3 * g], refs[-1])
        refs = refs[2 + 3 * g:-1]
    else:
        mix = refs[1][...]
        refs = refs[2:]
    wo_ref, gm_ref, w1_ref, w2_ref, gp_ref, wg_ref, p_ref, wp_ref, o_ref = refs
    x = x_ref[...] + jnp.dot(mix, wo_ref[...], preferred_element_type=F32)
    hn = _rms(x, gm_ref[...]).astype(BF16)
    acc = jnp.zeros_like(x)
    for c in range(D_FF // D_MODEL):
        cs = slice(c * D_MODEL, (c + 1) * D_MODEL)
        a = jnp.dot(hn, w1_ref[:, cs], preferred_element_type=F32)
        a = jnp.square(jnp.maximum(a, 0.0)).astype(BF16)
        acc = acc + jnp.dot(a, w2_ref[cs, :], preferred_element_type=F32)
    x = x + acc
    gate = jax.nn.sigmoid(jnp.dot(_rms(x, gp_ref[...]).astype(BF16), wg_ref[...],
                                  preferred_element_type=F32))
    emb = jnp.dot(p_ref[...].astype(BF16), wp_ref[...], preferred_element_type=F32)
    o_ref[...] = x + gate * emb


def _post_mixer(x, mix, wo, gm, w1, w2, gp, wg, p, layer, wp):
    s = x.shape[0]
    row = lambda width: pl.BlockSpec((ROW_TILE, width), lambda i: (i, 0))
    merge = isinstance(mix, tuple)
    if merge:
        sums, maxs, dens, expand = mix
        mix_args = [*sums, *maxs, *dens, expand]
        mix_specs = [pl.BlockSpec((N_SLABS, u.shape[1], ROW_TILE // u.shape[1], LANES),
                                  lambda i: (0, 0, i, 0)) for u in sums]
        mix_specs += [pl.BlockSpec((v.shape[0], ROW_TILE // v.shape[0], LANES), lambda i: (0, i, 0))
                      for v in (*maxs, *dens)]
        mix_specs.append(_resident(expand.shape))
        scratch = [pltpu.VMEM((N_SLABS + 2, ROW_TILE, LANES), F32)]
    else:
        mix_args = [mix]
        mix_specs = [row(mix.shape[1])]
        scratch = []
    return pl.pallas_call(
        functools.partial(_post_kernel, merge=merge),
        grid=(s // ROW_TILE,),
        in_specs=[row(D_MODEL), *mix_specs, _resident(wo.shape), _resident(gm.shape),
                  _resident(w1.shape), _resident(w2.shape), _resident(gp.shape),
                  _resident(wg.shape),
                  pl.BlockSpec((None, ROW_TILE, PLE_DIM), lambda i: (layer, i, 0)),
                  _resident(wp.shape)],
        out_specs=row(D_MODEL),
        out_shape=jax.ShapeDtypeStruct((s, D_MODEL), F32),
        scratch_shapes=scratch,
        compiler_params=_params(1),
        name="post_mixer",
    )(x, *mix_args, wo, gm, w1, w2, gp, wg, p, wp)


HALO = SUBLANES


def _up_conv_kernel(x_ref, g_ref, wup_ref, cw_ref, cb_ref, sk_ref, wq_ref, wk_ref, wv_ref, wg_ref,
                    bg_ref, q_ref, k_ref, v_ref, skx_ref, zg_ref, gt_ref, halo_sc, ext_sc):
    @pl.when(pl.program_id(0) == 0)
    def _():
        halo_sc[...] = jnp.zeros_like(halo_sc)

    hn = _rms(x_ref[...], g_ref[...]).astype(BF16)
    tm = hn.shape[0]
    ext_sc[:HALO, :] = halo_sc[...]
    for c in range(B_INNER // D_MODEL):
        cs = slice(c * D_MODEL, (c + 1) * D_MODEL)
        ext_sc[HALO:, cs] = jnp.dot(hn, wup_ref[:, cs], preferred_element_type=F32)
    halo_sc[...] = ext_sc[tm:, :]
    for c in range(B_INNER // D_MODEL):
        cs = slice(c * D_MODEL, (c + 1) * D_MODEL)
        z = jnp.dot(hn, wup_ref[:, B_INNER + c * D_MODEL:B_INNER + (c + 1) * D_MODEL],
                    preferred_element_type=F32)
        zg_ref[:, cs] = jax.nn.silu(z)
    gates = jnp.broadcast_to(bg_ref[...], gt_ref.shape)
    for t in range(B_INNER // MXU_TILE):
        ts = slice(t * MXU_TILE, (t + 1) * MXU_TILE)
        xm = ext_sc[HALO:, ts]
        y = cb_ref[:, ts] + cw_ref[B_CONV - 1:B_CONV, ts] * xm
        for j in range(1, B_CONV):
            y = y + cw_ref[B_CONV - 1 - j:B_CONV - j, ts] * ext_sc[pl.ds(HALO - j, tm), ts]
        xc = jax.nn.silu(y)
        skx_ref[:, ts] = sk_ref[:, ts] * xc
        xcb = xc.astype(BF16)
        qf = jnp.dot(xcb, wq_ref[t], preferred_element_type=F32)
        kf = jnp.dot(xcb, wk_ref[t], preferred_element_type=F32)
        vf = jnp.dot(xm.astype(BF16), wv_ref[t], preferred_element_type=F32)
        qb, kb, vb = qf.astype(BF16), kf.astype(BF16), vf.astype(BF16)
        q_ref[:, ts] = qb
        k_ref[:, ts] = (kf * B_HEAD_DIM ** -0.5).astype(BF16)
        v_ref[:, ts] = vb
        for part, val in enumerate((qb, kb, vb)):
            ws = slice(part * B_INNER + t * MXU_TILE, part * B_INNER + (t + 1) * MXU_TILE)
            gates = gates + jnp.dot(val, wg_ref[ws, :], preferred_element_type=F32)
    gt_ref[...] = gates


def _up_conv(x, g, wup, cw, cb, sk, wq, wk, wv, wg, bg):
    s = x.shape[0]
    row = lambda: pl.BlockSpec((ROW_TILE, B_INNER), lambda i: (i, 0))
    return pl.pallas_call(
        _up_conv_kernel,
        grid=(s // ROW_TILE,),
        in_specs=[pl.BlockSpec((ROW_TILE, D_MODEL), lambda i: (i, 0)),
                  _resident(g.shape), _resident(wup.shape),
                  _resident(cw.shape), _resident(cb.shape), _resident(sk.shape),
                  _resident(wq.shape), _resident(wk.shape), _resident(wv.shape),
                  _resident(wg.shape), _resident(bg.shape)],
        out_specs=[row(), row(), row(), row(), row(),
                   pl.BlockSpec((ROW_TILE, LANES), lambda i: (i, 0))],
        out_shape=[jax.ShapeDtypeStruct((s, B_INNER), BF16)] * 3
        + [jax.ShapeDtypeStruct((s, B_INNER), F32)] * 2 + [jax.ShapeDtypeStruct((s, LANES), F32)],
        scratch_shapes=[pltpu.VMEM((HALO, B_INNER), F32),
                        pltpu.VMEM((HALO + ROW_TILE, B_INNER), F32)],
        compiler_params=_params(1),
        name="up_conv_qkv_gates",
    )(x, g, wup, cw, cb, sk, wq, wk, wv, wg, bg)


N_COLS = B_HEAD_DIM + LANES


def _mlstm_kernel(q_ref, k_ref, v_ref, g_ref, skx_ref, zg_ref, tri_ref, hg_ref, *rest, n_riders):
    o_ref = rest[n_riders]
    c_sc, cb_sc, m_sc = rest[2 * n_riders + 1:]
    _run_riders(rest[:n_riders], rest[n_riders + 1:2 * n_riders + 1])

    @pl.when(pl.program_id(0) == 0)
    def _():
        c_sc[...] = jnp.zeros_like(c_sc)
        cb_sc[...] = jnp.zeros_like(cb_sc)
        m_sc[...] = jnp.full_like(m_sc, NEG_INF)

    chunk = q_ref.shape[0]
    g_all = g_ref[...]
    lf_all = jax.nn.log_sigmoid(g_all)
    b_all = jnp.dot(tri_ref[...], lf_all, preferred_element_type=F32,
                    precision=lax.Precision.HIGHEST)
    b_t = b_all.T
    g_t = g_all.T
    row = lax.broadcasted_iota(jnp.int32, (chunk, chunk), 0)
    col = lax.broadcasted_iota(jnp.int32, (chunk, chunk), 1)
    causal = col <= row
    ones = jnp.ones((chunk, LANES), BF16)
    for h in range(B_HEADS):
        hs = slice(h * B_HEAD_DIM, (h + 1) * B_HEAD_DIM)
        i_col = g_all[:, h:h + 1]
        b_col = b_all[:, B_HEADS + h:B_HEADS + h + 1]
        i_row = g_t[h:h + 1, :]
        b_row = b_t[B_HEADS + h:B_HEADS + h + 1, :]
        m_prev = m_sc[h, 0:1, 0:1]
        dmat = jnp.where(causal, b_col - b_row + i_row, NEG_INF)
        inter = b_col + m_prev
        m_t = jnp.maximum(inter, jnp.max(dmat, axis=-1, keepdims=True))
        qh = q_ref[:, hs]
        kh = k_ref[:, hs]
        vh = v_ref[:, hs]
        s = lax.dot_general(qh, kh, NT_DIMS, preferred_element_type=F32) * jnp.exp(dmat - m_t)
        sc = jnp.exp(inter - m_t)
        qc = jnp.dot(qh, cb_sc[h], preferred_element_type=F32)
        num = sc * qc[:, :B_HEAD_DIM] + jnp.dot(s.astype(BF16), vh, preferred_element_type=F32)
        den = sc * qc[:, B_HEAD_DIM:B_HEAD_DIM + 1] + jnp.sum(s, axis=-1, keepdims=True)
        hv = num / jnp.maximum(jnp.abs(den), jnp.exp(-m_t))
        b_last = b_col[chunk - 1:chunk, :]
        g_col = b_last - b_col + i_col
        m_new = jnp.maximum(b_last + m_prev, jnp.max(g_col, axis=0, keepdims=True))
        decay = jnp.exp(b_last + m_prev - m_new)
        wkb = (kh.astype(F32) * jnp.exp(g_col - m_new)).astype(BF16)
        vx = jnp.concatenate([vh, ones], axis=1)
        for rb in range(B_HEAD_DIM // LANES):
            rs = slice(rb * LANES, (rb + 1) * LANES)
            c_new = decay * c_sc[h, rs, :] + lax.dot_general(wkb[:, rs], vx, TN_DIMS,
                                                             preferred_element_type=F32)
            c_sc[h, rs, :] = c_new
            cb_sc[h, rs, :] = c_new.astype(BF16)
        m_sc[h] = jnp.broadcast_to(m_new, m_sc.shape[1:])
        o_ref[:, hs] = ((_rms(hv, hg_ref[:, hs]) + skx_ref[:, hs]) * zg_ref[:, hs]).astype(BF16)


def _mlstm_scan(q, k, v, gates, skx, zg, tri, hg, riders):
    s = q.shape[0]
    blk = lambda: pl.BlockSpec((SCAN_CHUNK, B_INNER), lambda c: (c, 0))
    r_in, r_out, r_shape = _rider_specs(riders, s // SCAN_CHUNK, lambda c: c)
    outs = pl.pallas_call(
        functools.partial(_mlstm_kernel, n_riders=len(riders)),
        grid=(s // SCAN_CHUNK,),
        in_specs=[blk(), blk(), blk(), pl.BlockSpec((SCAN_CHUNK, LANES), lambda c: (c, 0)),
                  blk(), blk(), _resident(tri.shape), _resident(hg.shape), *r_in],
        out_specs=[blk(), *r_out],
        out_shape=[jax.ShapeDtypeStruct((s, B_INNER), BF16), *r_shape],
        scratch_shapes=[pltpu.VMEM((B_HEADS, B_HEAD_DIM, N_COLS), F32),
                        pltpu.VMEM((B_HEADS, B_HEAD_DIM, N_COLS), BF16),
                        pltpu.VMEM((B_HEADS, SUBLANES, LANES), F32)],
        compiler_params=_params(1),
        name="mlstm_scan",
    )(q, k, v, gates, skx, zg, tri, hg, *[w for w, _ in riders])
    return outs[0], outs[1:]


def _block_diag_tiles(w):
    nblk, blk, _ = w.shape
    rows = w.reshape(nblk * blk // MXU_TILE, MXU_TILE, blk)
    idx = jnp.arange(MXU_TILE) // blk
    same_block = (idx[:, None] == idx[None, :]).astype(w.dtype)
    return (jnp.tile(rows, (1, 1, MXU_TILE // blk)) * same_block[None]).astype(BF16)


def _row(v):
    return v.reshape(1, -1).astype(F32)


def kernel(x, p, a_norm, a_w_qkv, a_q_gain, a_k_gain, a_w_o, b_norm, b_w_up, b_conv_w, b_conv_b,
           b_w_q, b_w_k, b_w_v, b_w_gate, b_b_gate, b_h_gain, b_skip, b_w_down, mlp_norm, mlp_w1,
           mlp_w2, ple_norm, ple_w_gate, ple_w_proj):
    bsz, seq, _ = x.shape
    assert bsz == 1
    xs = x.reshape(seq, D_MODEL)
    ps = p.reshape(p.shape[0], seq, PLE_DIM)

    qg = jnp.tile(a_q_gain[0], (1, A_HEADS)).reshape(A_N_GROUPS, 1, D_MODEL) * A_HEAD_DIM ** -0.5
    kg = jnp.tile(a_k_gain[0], (1, A_HEADS)).reshape(A_N_GROUPS, 1, D_MODEL)
    seg = np.arange(MXU_TILE) // A_HEAD_DIM
    ones = (seg[:, None] == seg[None, :]).astype(BF16)
    low = np.broadcast_to(np.arange(LANES) < A_HEAD_DIM, (Q_BLOCK, LANES))
    hmask = np.stack([low, ~low]).astype(BF16)
    expand = (np.arange(LANES)[:, None] == np.arange(D_MODEL)[None, :] // A_HEAD_DIM).astype(BF16)
    qkv_groups = _qkv_proj(xs, _row(a_norm[0]), a_w_qkv[0].astype(BF16), qg, kg, ones)
    riders = ([(mlp_w1, 0), (a_w_o, 0)],
              [(mlp_w2, 0), (ple_w_gate, 0), (ple_w_proj, 0)],
              [(b_w_up, 0), (ple_w_proj, 1)])
    stats, cast = zip(*[_attention_group(qkv_groups[grp], grp, dil, _attn_bias(dil), hmask,
                                         riders[grp]) for grp, (_, dil) in enumerate(A_GROUPS)])
    sums, maxs, dens = zip(*stats)
    (w1_0, wo_0), (w2_0, wg_0, wp_0), (wup, wp_1) = cast
    xs = _post_mixer(xs, (sums, maxs, dens, expand), wo_0, _row(mlp_norm[0]), w1_0, w2_0,
                     _row(ple_norm[0]), wg_0, ps, 0, wp_0)

    wg = jnp.zeros((3 * B_INNER, LANES), F32).at[:, :2 * B_HEADS].set(b_w_gate[0]).astype(BF16)
    bg = jnp.zeros((1, LANES), F32).at[0, :2 * B_HEADS].set(b_b_gate[0])
    q, k, v, skx, zg, gates = _up_conv(xs, _row(b_norm[0]), wup, b_conv_w[0],
                                       _row(b_conv_b[0]), _row(b_skip[0]),
                                       _block_diag_tiles(b_w_q[0]), _block_diag_tiles(b_w_k[0]),
                                       _block_diag_tiles(b_w_v[0]), wg, bg)
    tri = (np.arange(SCAN_CHUNK)[:, None] >= np.arange(SCAN_CHUNK)[None, :]).astype(np.float32)
    mix, (wo_1, w1_1, w2_1, wg_1) = _mlstm_scan(
        q, k, v, gates, skx, zg, tri, _row(b_h_gain[0]),
        [(b_w_down, 0), (mlp_w1, 1), (mlp_w2, 1), (ple_w_gate, 1)])
    xs = _post_mixer(xs, mix, wo_1, _row(mlp_norm[1]), w1_1, w2_1, _row(ple_norm[1]), wg_1, ps, 1,
                     wp_1)
    return xs.reshape(bsz, seq, D_MODEL)
```

```python
import functools

import jax
import jax.numpy as jnp
import numpy as np
from jax import lax
from jax.experimental import pallas as pl
from jax.experimental.pallas import tpu as pltpu

F32 = jnp.float32
BF16 = jnp.bfloat16

D_MODEL = 1024
A_HEADS = 16
A_HEAD_DIM = D_MODEL // A_HEADS
A_GROUPS = ((128, 1), (512, 4), (2048, 16))
A_N_GROUPS = len(A_GROUPS)
Q_BLOCK = 128
TILES_PER_STEP = 8
MAX_STORE_STRIDE = 4
B_INNER = 2 * D_MODEL
B_HEADS = 4
B_HEAD_DIM = B_INNER // B_HEADS
B_CONV = 4
SCAN_CHUNK = 256
D_FF = 4 * D_MODEL
PLE_DIM = 256
EPS = 1e-6
NEG_INF = -1e30

MXU_TILE = 256
LANES = 128
SUBLANES = 8
BF16_ROWS = 16
N_SLABS = D_MODEL // LANES
VMEM_LIMIT_BYTES = 60000 * 1024
ROW_TILE = 512
POST_TILES_PER_STEP = 2

NT_DIMS = (((1,), (1,)), ((), ()))
TN_DIMS = (((0,), (0,)), ((), ()))


def _params(n_axes):
    return pltpu.CompilerParams(dimension_semantics=("arbitrary",) * n_axes,
                                vmem_limit_bytes=VMEM_LIMIT_BYTES)


def _resident(shape):
    return pl.BlockSpec(shape, lambda *_: (0,) * len(shape), pipeline_mode=pl.Buffered(1))


def _rms(x, g):
    ms = jnp.mean(x * x, axis=-1, keepdims=True)
    return x * lax.rsqrt(ms + EPS) * g


def _rider_specs(jobs, n_steps, step_of):
    in_specs, out_specs, out_shape = [], [], []
    for w, layer in jobs:
        _, r, c = w.shape
        rows = r // n_steps
        assert rows * n_steps == r and rows % BF16_ROWS == 0, (w.shape, n_steps)
        in_specs.append(pl.BlockSpec((None, rows, c),
                                     lambda *g, layer=layer: (layer, step_of(*g), 0)))
        out_specs.append(pl.BlockSpec((rows, c), lambda *g: (step_of(*g), 0)))
        out_shape.append(jax.ShapeDtypeStruct((r, c), BF16))
    return in_specs, out_specs, out_shape


def _run_riders(src_refs, dst_refs):
    for src, dst in zip(src_refs, dst_refs):
        dst[...] = src[...].astype(BF16)


def _qkv_kernel(x_ref, g_ref, w_ref, qg_ref, kg_ref, ones_ref, *rest):
    out_refs, hn_sc = rest[:A_N_GROUPS], rest[A_N_GROUPS]
    hn = _rms(x_ref[...], g_ref[...])
    tm = hn.shape[0]
    for j in range(N_SLABS):
        hn_sc[j] = hn[:, j * LANES:(j + 1) * LANES]
    for grp, (_, dil) in enumerate(A_GROUPS):
        o_ref = out_refs[grp]
        n = tm // dil
        if dil == 1:
            hg = hn.astype(BF16)
        else:
            hg = jnp.concatenate(
                [jnp.concatenate([hn_sc[j, pl.ds(r, n, stride=dil), :] for j in range(N_SLABS)],
                                 axis=1) for r in range(dil)], axis=0).astype(BF16)
        for kind in range(3):
            c = 3 * grp + kind
            y = jnp.dot(hg, w_ref[:, c * D_MODEL:(c + 1) * D_MODEL], preferred_element_type=F32)
            if kind < 2:
                y2 = (y * y).astype(BF16)
                parts = [jnp.dot(y2[:, t * MXU_TILE:(t + 1) * MXU_TILE], ones_ref[...],
                                 preferred_element_type=F32) for t in range(D_MODEL // MXU_TILE)]
                ss = jnp.concatenate(parts, axis=1)
                gain = qg_ref[grp] if kind == 0 else kg_ref[grp]
                y = y * lax.rsqrt(ss * (1.0 / A_HEAD_DIM) + EPS) * gain
            yb = y.astype(BF16)
            ks = slice(kind * D_MODEL, (kind + 1) * D_MODEL)
            for r in range(dil):
                o_ref[r, :, ks] = yb[r * n:(r + 1) * n]


def _qkv_proj(x, g, w, qg, kg, ones):
    s = x.shape[0]
    out_specs, out_shape = [], []
    for _, dil in A_GROUPS:
        out_specs.append(pl.BlockSpec((dil, ROW_TILE // dil, 3 * D_MODEL), lambda i: (0, i, 0)))
        out_shape.append(jax.ShapeDtypeStruct((dil, s // dil, 3 * D_MODEL), BF16))
    return pl.pallas_call(
        _qkv_kernel,
        grid=(s // ROW_TILE,),
        in_specs=[pl.BlockSpec((ROW_TILE, D_MODEL), lambda i: (i, 0)),
                  _resident(g.shape), _resident(w.shape), _resident(qg.shape),
                  _resident(kg.shape), _resident(ones.shape)],
        out_specs=out_specs,
        out_shape=out_shape,
        scratch_shapes=[pltpu.VMEM((N_SLABS, ROW_TILE, LANES), F32)],
        compiler_params=_params(1),
        name="qkv_proj",
    )(x, g, w, qg, kg, ones)


def _attn_kernel(q_ref, kp_ref, kc_ref, vp_ref, vc_ref, bias_ref, hmask_ref, *rest, dil, n_riders):
    o_ref, m_ref, l_ref = rest[n_riders:n_riders + 3]
    _run_riders(rest[:n_riders], rest[n_riders + 3:])
    n_res = q_ref.shape[0]
    n_tiles = q_ref.shape[1] // Q_BLOCK
    planes = m_ref.shape[0]
    stride = dil // planes
    lane = lax.broadcasted_iota(jnp.int32, (Q_BLOCK, LANES), 1)
    lo = lane < A_HEAD_DIM
    ones = jnp.ones((2 * Q_BLOCK, LANES), BF16)
    for sub in range(n_res):
        plane = sub % planes
        pos = pl.program_id(1) * (n_res // planes) + sub // planes
        for tile in range(n_tiles):
            cur = slice(tile * Q_BLOCK, (tile + 1) * Q_BLOCK)
            both = slice((tile - 1) * Q_BLOCK, (tile + 1) * Q_BLOCK)
            if stride > 1:
                rows = pl.ds(tile * Q_BLOCK * stride + pos, Q_BLOCK, stride=stride)
            else:
                rows = cur
            if tile == 0:
                sel = (pl.program_id(0) == 0).astype(jnp.int32)
            else:
                sel = 0
            m_tile = jnp.zeros((Q_BLOCK, LANES), F32)
            l_tile = jnp.ones((Q_BLOCK, LANES), F32)
            for pair in range(A_HEADS // 2):
                cs = slice(pair * LANES, (pair + 1) * LANES)
                q2 = q_ref[sub, cur, cs]
                if tile == 0:
                    kcat = jnp.concatenate([kp_ref[sub, :, cs], kc_ref[sub, cur, cs]], axis=0)
                    vcat = jnp.concatenate([vp_ref[sub, :, cs], vc_ref[sub, cur, cs]], axis=0)
                else:
                    kcat = kc_ref[sub, both, cs]
                    vcat = vc_ref[sub, both, cs]
                qq = jnp.concatenate([q2 * hmask_ref[0], q2 * hmask_ref[1]], axis=0)
                s = lax.dot_general(qq, kcat, NT_DIMS, preferred_element_type=F32)
                s = s + bias_ref[sel, pair]
                m = jnp.max(s, axis=-1, keepdims=True)
                p = jnp.exp((s - m).astype(BF16))
                ul = jnp.dot(p, jnp.concatenate([vcat, ones], axis=1), preferred_element_type=F32)
                u, l = ul[:, :LANES], ul[:, LANES:]
                for hh in range(2):
                    hr = slice(hh * Q_BLOCK, (hh + 1) * Q_BLOCK)
                    m_tile = jnp.where(lane == 2 * pair + hh, m[hr], m_tile)
                    l_tile = jnp.where(lane == 2 * pair + hh, l[hr], l_tile)
                o_ref[pair, plane, rows, :] = jnp.where(lo, u[:Q_BLOCK], u[Q_BLOCK:])
            m_ref[plane, rows, :] = m_tile
            l_ref[plane, rows, :] = l_tile


def _attn_bias(dil):
    slopes = np.asarray([2.0 ** (-8.0 * (h + 1) / A_HEADS) for h in range(A_HEADS)], np.float32)
    row = np.arange(Q_BLOCK)[:, None]
    col = np.arange(Q_BLOCK)[None, :]

    def table(steps, valid):
        dist = (steps * dil).astype(np.float32)
        return np.where(valid[None], -(slopes[:, None, None] * dist[None]), np.float32(NEG_INF))

    prev = table(Q_BLOCK + row - col, col >= row)
    cur = table(row - col, col <= row)
    normal = np.concatenate([prev, cur], axis=-1)
    firstb = np.concatenate([np.full_like(prev, NEG_INF), cur], axis=-1)
    tables = np.stack([normal, firstb]).astype(np.float32)
    return tables.reshape(2, A_HEADS // 2, 2 * Q_BLOCK, 2 * Q_BLOCK)


def _attention_group(qkv_g, grp, dil, bias, hmask, riders):
    _, sd, _ = qkv_g.shape
    s = sd * dil
    n_res = min(TILES_PER_STEP, dil)
    n_tiles = TILES_PER_STEP // n_res
    blk = n_tiles * Q_BLOCK
    planes = max(1, dil // MAX_STORE_STRIDE)
    assert n_res % planes == 0
    prow = blk * dil // planes

    def col_spec(off, prev):
        if prev:
            return pl.BlockSpec((n_res, Q_BLOCK, D_MODEL),
                                lambda b, r: (r, jnp.maximum(b * n_tiles - 1, 0), off))
        return pl.BlockSpec((n_res, blk, D_MODEL), lambda b, r: (r, b, off))

    grid = (sd // blk, dil // n_res)
    r_in, r_out, r_shape = _rider_specs(riders, grid[0] * grid[1], lambda b, r: b * grid[1] + r)
    outs = pl.pallas_call(
        functools.partial(_attn_kernel, dil=dil, n_riders=len(riders)),
        grid=grid,
        in_specs=[col_spec(0, False), col_spec(1, True), col_spec(1, False),
                  col_spec(2, True), col_spec(2, False),
                  _resident(bias.shape), _resident(hmask.shape), *r_in],
        out_specs=[pl.BlockSpec((N_SLABS, planes, prow, LANES), lambda b, r: (0, 0, b, 0)),
                   pl.BlockSpec((planes, prow, LANES), lambda b, r: (0, b, 0)),
                   pl.BlockSpec((planes, prow, LANES), lambda b, r: (0, b, 0)), *r_out],
        out_shape=[jax.ShapeDtypeStruct((N_SLABS, planes, s // planes, LANES), F32),
                   jax.ShapeDtypeStruct((planes, s // planes, LANES), F32),
                   jax.ShapeDtypeStruct((planes, s // planes, LANES), F32), *r_shape],
        compiler_params=_params(2),
        name=f"dilated_attn_g{grp}",
    )(qkv_g, qkv_g, qkv_g, qkv_g, qkv_g, bias, hmask, *[w for w, _ in riders])
    return outs[:3], outs[3:]


def _natural_rows(ref, lead, stage_ref):
    planes = ref.shape[-3]
    if planes == 1:
        return ref[(*lead, 0)]
    for b in range(planes):
        stage_ref[pl.ds(b, ref.shape[-2], stride=planes), :] = ref[(*lead, b)]
    return stage_ref[...]


def _merge_groups(u_refs, m_refs, l_refs, expand_ref, stage_ref):
    ms = [_natural_rows(r, (), stage_ref.at[N_SLABS]) for r in m_refs]
    ls = [_natural_rows(r, (), stage_ref.at[N_SLABS + 1]) for r in l_refs]
    top = functools.reduce(jnp.maximum, ms)
    es = [jnp.exp(v - top) for v in ms]
    den = functools.reduce(jnp.add, [e * l for e, l in zip(es, ls)])
    merged = None
    for u_ref, e in zip(u_refs, es):
        wide = jnp.dot((e / den).astype(BF16), expand_ref[...], preferred_element_type=F32)
        u = jnp.concatenate([_natural_rows(u_ref, (j,), stage_ref.at[j]) for j in range(N_SLABS)],
                            axis=1)
        merged = wide * u if merged is None else merged + wide * u
    return merged.astype(BF16)


def _post_kernel(*refs, merge):
    x_ref = refs[0]
    if merge:
        g = A_N_GROUPS
        merged = _merge_groups(refs[1:1 + g], refs[1 + g:1 + 2 * g], refs[1 + 2 * g:1 + 3 * g],
                               refs[1 + 3 * g], refs[-1])
        mix = lambda rows: merged
        refs = refs[2 + 3 * g:-1]
    else:
        mix_ref = refs[1]
        mix = lambda rows: mix_ref[rows, :]
        refs = refs[2:]
    wo_ref, gm_ref, w1_ref, w2_ref, gp_ref, wg_ref, p_ref, wp_ref, o_ref = refs
    tiles = [pl.ds(t * ROW_TILE, ROW_TILE) for t in range(x_ref.shape[0] // ROW_TILE)]
    hns = []
    for rows in tiles:
        o_ref[rows, :] = x_ref[rows, :] + jnp.dot(mix(rows), wo_ref[...],
                                                  preferred_element_type=F32)
        hns.append(_rms(o_ref[rows, :], gm_ref[...]).astype(BF16))
    for c in range(D_FF // D_MODEL):
        cs = slice(c * D_MODEL, (c + 1) * D_MODEL)
        for rows, hn in zip(tiles, hns):
            a = jnp.dot(hn, w1_ref[:, cs], preferred_element_type=F32)
            a = jnp.square(jnp.maximum(a, 0.0)).astype(BF16)
            o_ref[rows, :] += jnp.dot(a, w2_ref[cs, :], preferred_element_type=F32)
    for rows in tiles:
        x = o_ref[rows, :]
        gate = jax.nn.sigmoid(jnp.dot(_rms(x, gp_ref[...]).astype(BF16), wg_ref[...],
                                      preferred_element_type=F32))
        emb = jnp.dot(p_ref[rows, :].astype(BF16), wp_ref[...], preferred_element_type=F32)
        o_ref[rows, :] = x + gate * emb


def _post_mixer(x, mix, wo, gm, w1, w2, gp, wg, p, layer, wp):
    s = x.shape[0]
    merge = isinstance(mix, tuple)
    step_rows = ROW_TILE if merge else POST_TILES_PER_STEP * ROW_TILE
    row = lambda width: pl.BlockSpec((step_rows, width), lambda i: (i, 0))
    if merge:
        sums, maxs, dens, expand = mix
        mix_args = [*sums, *maxs, *dens, expand]
        mix_specs = [pl.BlockSpec((N_SLABS, u.shape[1], ROW_TILE // u.shape[1], LANES),
                                  lambda i: (0, 0, i, 0)) for u in sums]
        mix_specs += [pl.BlockSpec((v.shape[0], ROW_TILE // v.shape[0], LANES), lambda i: (0, i, 0))
                      for v in (*maxs, *dens)]
        mix_specs.append(_resident(expand.shape))
        scratch = [pltpu.VMEM((N_SLABS + 2, ROW_TILE, LANES), F32)]
    else:
        mix_args = [mix]
        mix_specs = [row(mix.shape[1])]
        scratch = []
    return pl.pallas_call(
        functools.partial(_post_kernel, merge=merge),
        grid=(s // step_rows,),
        in_specs=[row(D_MODEL), *mix_specs, _resident(wo.shape), _resident(gm.shape),
                  _resident(w1.shape), _resident(w2.shape), _resident(gp.shape),
                  _resident(wg.shape),
                  pl.BlockSpec((None, step_rows, PLE_DIM), lambda i: (layer, i, 0)),
                  _resident(wp.shape)],
        out_specs=row(D_MODEL),
        out_shape=jax.ShapeDtypeStruct((s, D_MODEL), F32),
        scratch_shapes=scratch,
        compiler_params=_params(1),
        name="post_mixer",
    )(x, *mix_args, wo, gm, w1, w2, gp, wg, p, wp)


HALO = SUBLANES


def _up_conv_kernel(x_ref, g_ref, wup_ref, cw_ref, cb_ref, sk_ref, wq_ref, wk_ref, wv_ref, wg_ref,
                    bg_ref, q_ref, k_ref, v_ref, skx_ref, zg_ref, gt_ref, halo_sc, ext_sc):
    @pl.when(pl.program_id(0) == 0)
    def _():
        halo_sc[...] = jnp.zeros_like(halo_sc)

    hn = _rms(x_ref[...], g_ref[...]).astype(BF16)
    tm = hn.shape[0]
    ext_sc[:HALO, :] = halo_sc[...]
    for c in range(B_INNER // D_MODEL):
        cs = slice(c * D_MODEL, (c + 1) * D_MODEL)
        ext_sc[HALO:, cs] = jnp.dot(hn, wup_ref[:, cs], preferred_element_type=F32)
    halo_sc[...] = ext_sc[tm:, :]
    for c in range(B_INNER // D_MODEL):
        cs = slice(c * D_MODEL, (c + 1) * D_MODEL)
        z = jnp.dot(hn, wup_ref[:, B_INNER + c * D_MODEL:B_INNER + (c + 1) * D_MODEL],
                    preferred_element_type=F32)
        zg_ref[:, cs] = jax.nn.silu(z)
    gates = jnp.broadcast_to(bg_ref[...], gt_ref.shape)
    for t in range(B_INNER // MXU_TILE):
        ts = slice(t * MXU_TILE, (t + 1) * MXU_TILE)
        xm = ext_sc[HALO:, ts]
        y = cb_ref[:, ts] + cw_ref[B_CONV - 1:B_CONV, ts] * xm
        for j in range(1, B_CONV):
            y = y + cw_ref[B_CONV - 1 - j:B_CONV - j, ts] * ext_sc[pl.ds(HALO - j, tm), ts]
        xc = jax.nn.silu(y)
        skx_ref[:, ts] = sk_ref[:, ts] * xc
        xcb = xc.astype(BF16)
        qf = jnp.dot(xcb, wq_ref[t], preferred_element_type=F32)
        kf = jnp.dot(xcb, wk_ref[t], preferred_element_type=F32)
        vf = jnp.dot(xm.astype(BF16), wv_ref[t], preferred_element_type=F32)
        qb, kb, vb = qf.astype(BF16), kf.astype(BF16), vf.astype(BF16)
        q_ref[:, ts] = qb
        k_ref[:, ts] = (kf * B_HEAD_DIM ** -0.5).astype(BF16)
        v_ref[:, ts] = vb
        for part, val in enumerate((qb, kb, vb)):
            ws = slice(part * B_INNER + t * MXU_TILE, part * B_INNER + (t + 1) * MXU_TILE)
            gates = gates + jnp.dot(val, wg_ref[ws, :], preferred_element_type=F32)
    gt_ref[...] = gates


def _up_conv(x, g, wup, cw, cb, sk, wq, wk, wv, wg, bg):
    s = x.shape[0]
    row = lambda: pl.BlockSpec((ROW_TILE, B_INNER), lambda i: (i, 0))
    return pl.pallas_call(
        _up_conv_kernel,
        grid=(s // ROW_TILE,),
        in_specs=[pl.BlockSpec((ROW_TILE, D_MODEL), lambda i: (i, 0)),
                  _resident(g.shape), _resident(wup.shape),
                  _resident(cw.shape), _resident(cb.shape), _resident(sk.shape),
                  _resident(wq.shape), _resident(wk.shape), _resident(wv.shape),
                  _resident(wg.shape), _resident(bg.shape)],
        out_specs=[row(), row(), row(), row(), row(),
                   pl.BlockSpec((ROW_TILE, LANES), lambda i: (i, 0))],
        out_shape=[jax.ShapeDtypeStruct((s, B_INNER), BF16)] * 3
        + [jax.ShapeDtypeStruct((s, B_INNER), F32)] * 2 + [jax.ShapeDtypeStruct((s, LANES), F32)],
        scratch_shapes=[pltpu.VMEM((HALO, B_INNER), F32),
                        pltpu.VMEM((HALO + ROW_TILE, B_INNER), F32)],
        compiler_params=_params(1),
        name="up_conv_qkv_gates",
    )(x, g, wup, cw, cb, sk, wq, wk, wv, wg, bg)


N_COLS = B_HEAD_DIM + LANES


def _mlstm_kernel(q_ref, k_ref, v_ref, g_ref, skx_ref, zg_ref, tri_ref, hg_ref, *rest, n_riders):
    o_ref = rest[n_riders]
    c_sc, cb_sc, m_sc = rest[2 * n_riders + 1:]
    _run_riders(rest[:n_riders], rest[n_riders + 1:2 * n_riders + 1])

    @pl.when(pl.program_id(0) == 0)
    def _():
        c_sc[...] = jnp.zeros_like(c_sc)
        cb_sc[...] = jnp.zeros_like(cb_sc)
        m_sc[...] = jnp.full_like(m_sc, NEG_INF)

    chunk = q_ref.shape[0]
    g_all = g_ref[...]
    lf_all = jax.nn.log_sigmoid(g_all)
    b_all = jnp.dot(tri_ref[...], lf_all, preferred_element_type=F32,
                    precision=lax.Precision.HIGHEST)
    b_t = b_all.T
    g_t = g_all.T
    row = lax.broadcasted_iota(jnp.int32, (chunk, chunk), 0)
    col = lax.broadcasted_iota(jnp.int32, (chunk, chunk), 1)
    causal = col <= row
    ones = jnp.ones((chunk, LANES), BF16)
    for h in range(B_HEADS):
        hs = slice(h * B_HEAD_DIM, (h + 1) * B_HEAD_DIM)
        i_col = g_all[:, h:h + 1]
        b_col = b_all[:, B_HEADS + h:B_HEADS + h + 1]
        i_row = g_t[h:h + 1, :]
        b_row = b_t[B_HEADS + h:B_HEADS + h + 1, :]
        m_prev = m_sc[h, 0:1, 0:1]
        dmat = jnp.where(causal, b_col - b_row + i_row, NEG_INF)
        inter = b_col + m_prev
        m_t = jnp.maximum(inter, jnp.max(dmat, axis=-1, keepdims=True))
        qh = q_ref[:, hs]
        kh = k_ref[:, hs]
        vh = v_ref[:, hs]
        s = lax.dot_general(qh, kh, NT_DIMS, preferred_element_type=F32) * jnp.exp(dmat - m_t)
        sc = jnp.exp(inter - m_t)
        qc = jnp.dot(qh, cb_sc[h], preferred_element_type=F32)
        num = sc * qc[:, :B_HEAD_DIM] + jnp.dot(s.astype(BF16), vh, preferred_element_type=F32)
        den = sc * qc[:, B_HEAD_DIM:B_HEAD_DIM + 1] + jnp.sum(s, axis=-1, keepdims=True)
        hv = num / jnp.maximum(jnp.abs(den), jnp.exp(-m_t))
        b_last = b_col[chunk - 1:chunk, :]
        g_col = b_last - b_col + i_col
        m_new = jnp.maximum(b_last + m_prev, jnp.max(g_col, axis=0, keepdims=True))
        decay = jnp.exp(b_last + m_prev - m_new)
        wkb = (kh.astype(F32) * jnp.exp(g_col - m_new)).astype(BF16)
        vx = jnp.concatenate([vh, ones], axis=1)
        for rb in range(B_HEAD_DIM // LANES):
            rs = slice(rb * LANES, (rb + 1) * LANES)
            c_new = decay * c_sc[h, rs, :] + lax.dot_general(wkb[:, rs], vx, TN_DIMS,
                                                             preferred_element_type=F32)
            c_sc[h, rs, :] = c_new
            cb_sc[h, rs, :] = c_new.astype(BF16)
        m_sc[h] = jnp.broadcast_to(m_new, m_sc.shape[1:])
        o_ref[:, hs] = ((_rms(hv, hg_ref[:, hs]) + skx_ref[:, hs]) * zg_ref[:, hs]).astype(BF16)


def _mlstm_scan(q, k, v, gates, skx, zg, tri, hg, riders):
    s = q.shape[0]
    blk = lambda: pl.BlockSpec((SCAN_CHUNK, B_INNER), lambda c: (c, 0))
    r_in, r_out, r_shape = _rider_specs(riders, s // SCAN_CHUNK, lambda c: c)
    outs = pl.pallas_call(
        functools.partial(_mlstm_kernel, n_riders=len(riders)),
        grid=(s // SCAN_CHUNK,),
        in_specs=[blk(), blk(), blk(), pl.BlockSpec((SCAN_CHUNK, LANES), lambda c: (c, 0)),
                  blk(), blk(), _resident(tri.shape), _resident(hg.shape), *r_in],
        out_specs=[blk(), *r_out],
        out_shape=[jax.ShapeDtypeStruct((s, B_INNER), BF16), *r_shape],
        scratch_shapes=[pltpu.VMEM((B_HEADS, B_HEAD_DIM, N_COLS), F32),
                        pltpu.VMEM((B_HEADS, B_HEAD_DIM, N_COLS), BF16),
                        pltpu.VMEM((B_HEADS, SUBLANES, LANES), F32)],
        compiler_params=_params(1),
        name="mlstm_scan",
    )(q, k, v, gates, skx, zg, tri, hg, *[w for w, _ in riders])
    return outs[0], outs[1:]


def _block_diag_tiles(w):
    nblk, blk, _ = w.shape
    rows = w.reshape(nblk * blk // MXU_TILE, MXU_TILE, blk)
    idx = jnp.arange(MXU_TILE) // blk
    same_block = (idx[:, None] == idx[None, :]).astype(w.dtype)
    return (jnp.tile(rows, (1, 1, MXU_TILE // blk)) * same_block[None]).astype(BF16)


def _row(v):
    return v.reshape(1, -1).astype(F32)


def kernel(x, p, a_norm, a_w_qkv, a_q_gain, a_k_gain, a_w_o, b_norm, b_w_up, b_conv_w, b_conv_b,
           b_w_q, b_w_k, b_w_v, b_w_gate, b_b_gate, b_h_gain, b_skip, b_w_down, mlp_norm, mlp_w1,
           mlp_w2, ple_norm, ple_w_gate, ple_w_proj):
    bsz, seq, _ = x.shape
    assert bsz == 1
    xs = x.reshape(seq, D_MODEL)
    ps = p.reshape(p.shape[0], seq, PLE_DIM)

    qg = jnp.tile(a_q_gain[0], (1, A_HEADS)).reshape(A_N_GROUPS, 1, D_MODEL) * A_HEAD_DIM ** -0.5
    kg = jnp.tile(a_k_gain[0], (1, A_HEADS)).reshape(A_N_GROUPS, 1, D_MODEL)
    seg = np.arange(MXU_TILE) // A_HEAD_DIM
    ones = (seg[:, None] == seg[None, :]).astype(BF16)
    low = np.broadcast_to(np.arange(LANES) < A_HEAD_DIM, (Q_BLOCK, LANES))
    hmask = np.stack([low, ~low]).astype(BF16)
    expand = (np.arange(LANES)[:, None] == np.arange(D_MODEL)[None, :] // A_HEAD_DIM).astype(BF16)
    qkv_groups = _qkv_proj(xs, _row(a_norm[0]), a_w_qkv[0].astype(BF16), qg, kg, ones)
    riders = ([(mlp_w1, 0), (a_w_o, 0)],
              [(mlp_w2, 0), (ple_w_gate, 0), (ple_w_proj, 0)],
              [(b_w_up, 0), (ple_w_proj, 1)])
    stats, cast = zip(*[_attention_group(qkv_groups[grp], grp, dil, _attn_bias(dil), hmask,
                                         riders[grp]) for grp, (_, dil) in enumerate(A_GROUPS)])
    sums, maxs, dens = zip(*stats)
    (w1_0, wo_0), (w2_0, wg_0, wp_0), (wup, wp_1) = cast
    xs = _post_mixer(xs, (sums, maxs, dens, expand), wo_0, _row(mlp_norm[0]), w1_0, w2_0,
                     _row(ple_norm[0]), wg_0, ps, 0, wp_0)

    wg = jnp.zeros((3 * B_INNER, LANES), F32).at[:, :2 * B_HEADS].set(b_w_gate[0]).astype(BF16)
    bg = jnp.zeros((1, LANES), F32).at[0, :2 * B_HEADS].set(b_b_gate[0])
    q, k, v, skx, zg, gates = _up_conv(xs, _row(b_norm[0]), wup, b_conv_w[0],
                                       _row(b_conv_b[0]), _row(b_skip[0]),
                                       _block_diag_tiles(b_w_q[0]), _block_diag_tiles(b_w_k[0]),
                                       _block_diag_tiles(b_w_v[0]), wg, bg)
    tri = (np.arange(SCAN_CHUNK)[:, None] >= np.arange(SCAN_CHUNK)[None, :]).astype(np.float32)
    mix, (wo_1, w1_1, w2_1, wg_1) = _mlstm_scan(
        q, k, v, gates, skx, zg, tri, _row(b_h_gain[0]),
        [(b_w_down, 0), (mlp_w1, 1), (mlp_w2, 1), (ple_w_gate, 1)])
    xs = _post_mixer(xs, mix, wo_1, _row(mlp_norm[1]), w1_1, w2_1, _row(ple_norm[1]), wg_1, ps, 1,
                     wp_1)
    return xs.reshape(bsz, seq, D_MODEL)
```
